```python
import math
import jax, jax.numpy as jnp
from jax import lax
import numpy as np

D_MODEL = 2048
BATCH = 16
SEQ = 2048
DEPTH = 1
DEC_BATCH = 32
DEC_SEQ = 8
PAST_LEN = 16384
PAGE_SIZE = 128

MIX_WIDTH = D_MODEL
DIFF_WIDTH = MIX_WIDTH // 2
SGU_WIDTH = MIX_WIDTH - DIFF_WIDTH
N_DIFF_HEADS = 8
V_DIM = DIFF_WIDTH // N_DIFF_HEADS
HEAD_DIM = V_DIM // 2
N_SGU_GROUPS = 4
SGU_GROUP_W = SGU_WIDTH // N_SGU_GROUPS
CHUNK = 128
QBLOCK = 128
PROJ_COLS = 3 * DIFF_WIDTH + 2 * SGU_WIDTH
N_BUCKETS = 32
MAX_DISTANCE = 128
N_EXPERT_GROUPS = 4
EXPERTS_PER_GROUP = 8
N_EXPERTS = N_EXPERT_GROUPS * EXPERTS_PER_GROUP
TOP_K_IN_GROUP = 2
D_EXPERT = 512
PLE_DIM = 256
LN_EPS = 1e-5
NEG_INF = -1e30
DEEPNORM_ALPHA = (2.0 * DEPTH) ** 0.25
DEEPNORM_BETA = (8.0 * DEPTH) ** -0.25

kernel_name = 'hymba_diffattn_sgu_hmoe_step'


def _layer_norm(x, g, b):
    xf = x.astype(jnp.float32)
    mu = jnp.mean(xf, -1, keepdims=True)
    var = jnp.mean(jnp.square(xf - mu), -1, keepdims=True)
    y = (xf - mu) * lax.rsqrt(var + LN_EPS) * g.astype(jnp.float32) + b.astype(jnp.float32)
    return y.astype(x.dtype)


def _t5_bias(rel_bias, dist):
    n = jnp.maximum(dist, 0)
    max_exact = N_BUCKETS // 2
    nf = jnp.maximum(n, max_exact).astype(jnp.float32)
    large = max_exact + (jnp.log(nf / max_exact) / math.log(MAX_DISTANCE / max_exact)
                         * (N_BUCKETS - max_exact)).astype(jnp.int32)
    large = jnp.minimum(large, N_BUCKETS - 1)
    bucket = jnp.where(n < max_exact, n, large)
    return jnp.moveaxis(rel_bias[bucket].astype(jnp.float32), -1, 0)


def _split_proj(z):
    lead = z.shape[:-1]
    q = z[..., :DIFF_WIDTH].reshape(*lead, N_DIFF_HEADS, 2 * HEAD_DIM)
    k = z[..., DIFF_WIDTH:2 * DIFF_WIDTH].reshape(*lead, N_DIFF_HEADS, 2 * HEAD_DIM)
    v = z[..., 2 * DIFF_WIDTH:3 * DIFF_WIDTH].reshape(*lead, N_DIFF_HEADS, V_DIM)
    u = z[..., 3 * DIFF_WIDTH:3 * DIFF_WIDTH + SGU_WIDTH]
    gv = z[..., 3 * DIFF_WIDTH + SGU_WIDTH:]
    return q, k, v, u, gv


def _diff_lambda(lq1, lk1, lq2, lk2, lam_init):
    f = lambda a: a.astype(jnp.float32)
    return jnp.exp(jnp.sum(f(lq1) * f(lk1))) - jnp.exp(jnp.sum(f(lq2) * f(lk2))) + lam_init


def _diff_attn_prompt(q, k, v, lam, rel_bias):
    bsz, t = q.shape[:2]
    qf = q.astype(jnp.float32) * HEAD_DIM ** -0.5
    kf = k.astype(jnp.float32)
    vf = v.astype(jnp.float32)
    q1, q2 = qf[..., :HEAD_DIM], qf[..., HEAD_DIM:]
    k1, k2 = kf[..., :HEAD_DIM], kf[..., HEAD_DIM:]
    kpos = jnp.arange(t)

    def block(i):
        start = i * QBLOCK
        q1b = lax.dynamic_slice_in_dim(q1, start, QBLOCK, axis=1)
        q2b = lax.dynamic_slice_in_dim(q2, start, QBLOCK, axis=1)
        dist = (start + jnp.arange(QBLOCK))[:, None] - kpos[None, :]
        bias = _t5_bias(rel_bias, dist)[None]
        causal = (dist >= 0)[None, None]
        s1 = jnp.where(causal, jnp.einsum('bqhd,bkhd->bhqk', q1b, k1) + bias, NEG_INF)
        s2 = jnp.where(causal, jnp.einsum('bqhd,bkhd->bhqk', q2b, k2) + bias, NEG_INF)
        w = jax.nn.softmax(s1, axis=-1) - lam * jax.nn.softmax(s2, axis=-1)
        return jnp.einsum('bhqk,bkhd->bqhd', w, vf)

    o = lax.map(block, jnp.arange(t // QBLOCK))
    return jnp.moveaxis(o, 0, 1).reshape(bsz, t, N_DIFF_HEADS, V_DIM)


def _online(st, s, v):
    m, l, acc = st
    m_new = jnp.maximum(m, jnp.max(s, -1))
    corr = jnp.exp(m - m_new)
    p = jnp.exp(s - m_new[..., None])
    return (m_new, l * corr + jnp.sum(p, -1),
            acc * corr[..., None] + jnp.einsum('bhqk,bkhd->bhqd', p, v))


def _diff_attn_sample(q, k_new, v_new, cache_k_l, cache_v_l, page_table, lam, rel_bias):
    bd, tq = q.shape[:2]
    n_pages = page_table.shape[1]
    past = n_pages * PAGE_SIZE
    qf = q.astype(jnp.float32) * HEAD_DIM ** -0.5
    q1, q2 = qf[..., :HEAD_DIM], qf[..., HEAD_DIM:]
    qpos = past + jnp.arange(tq)

    def init():
        return (jnp.full((bd, N_DIFF_HEADS, tq), NEG_INF, jnp.float32),
                jnp.zeros((bd, N_DIFF_HEADS, tq), jnp.float32),
                jnp.zeros((bd, N_DIFF_HEADS, tq, V_DIM), jnp.float32))

    def scores(kb, kpos):
        bias = _t5_bias(rel_bias, qpos[:, None] - kpos[None, :])[None]
        s1 = jnp.einsum('bqhd,bkhd->bhqk', q1, kb[..., :HEAD_DIM]) + bias
        s2 = jnp.einsum('bqhd,bkhd->bhqk', q2, kb[..., HEAD_DIM:]) + bias
        return s1, s2

    def page_step(carry, xs):
        j, phys = xs
        kb = cache_k_l[phys].astype(jnp.float32)
        vb = cache_v_l[phys].astype(jnp.float32)
        s1, s2 = scores(kb, j * PAGE_SIZE + jnp.arange(PAGE_SIZE))
        return (_online(carry[0], s1, vb), _online(carry[1], s2, vb)), None

    carry, _ = lax.scan(page_step, (init(), init()), (jnp.arange(n_pages), page_table.T))
    s1, s2 = scores(k_new.astype(jnp.float32), qpos)
    causal = (qpos[:, None] >= qpos[None, :])[None, None]
    vn = v_new.astype(jnp.float32)
    st1 = _online(carry[0], jnp.where(causal, s1, NEG_INF), vn)
    st2 = _online(carry[1], jnp.where(causal, s2, NEG_INF), vn)
    o = st1[2] / st1[1][..., None] - lam * (st2[2] / st2[1][..., None])
    return jnp.transpose(o, (0, 2, 1, 3))


def _diff_head_out(o, g, lam_init, dtype):
    of = o.astype(jnp.float32)
    of = of * lax.rsqrt(jnp.mean(jnp.square(of), -1, keepdims=True) + LN_EPS) * g.astype(jnp.float32)
    of = of * (1.0 - lam_init)
    return of.reshape(*o.shape[:-2], DIFF_WIDTH).astype(dtype)


def _sgu(u, gv, ln_g, ln_b, w_s, b_s):
    u = jax.nn.gelu(u)
    v = _layer_norm(jax.nn.gelu(gv), ln_g, ln_b)
    bsz, t = v.shape[:2]
    pad = (-t) % CHUNK
    nc = (t + pad) // CHUNK
    vp = jnp.pad(v, ((0, 0), (0, pad), (0, 0))).reshape(bsz, nc, CHUNK, N_SGU_GROUPS, SGU_GROUP_W)
    w_causal = jnp.where(jnp.tril(jnp.ones((CHUNK, CHUNK), bool))[None], w_s, 0.0)
    sv = jnp.einsum('gij,bcjge->bcige', w_causal, vp) + b_s.T[None, None, :, :, None]
    sv = sv.reshape(bsz, nc * CHUNK, SGU_WIDTH)[:, :t]
    return u * sv, v


def _hier_moe(h, w_rg, b_rg, w_re, b_re, w_gate, w_up, w_down):
    shp = h.shape
    x = h.reshape(-1, shp[-1])
    tok = x.shape[0]
    glog = (x @ w_rg + b_rg).astype(jnp.float32)
    g_idx = jnp.argmax(glog, -1)
    g_w = jnp.take_along_axis(jax.nn.softmax(glog, -1), g_idx[:, None], axis=1)
    elog = (x @ w_re + b_re).astype(jnp.float32).reshape(tok, N_EXPERT_GROUPS, EXPERTS_PER_GROUP)
    elog_g = jnp.take_along_axis(elog, g_idx[:, None, None], axis=1)[:, 0]
    top_v, top_i = lax.top_k(elog_g, TOP_K_IN_GROUP)
    top_w = jax.nn.softmax(top_v, -1) * g_w
    e_idx = g_idx[:, None] * EXPERTS_PER_GROUP + top_i
    combine = jnp.sum(jax.nn.one_hot(e_idx, N_EXPERTS, dtype=jnp.float32) * top_w[..., None], 1).astype(x.dtype)
    y = jnp.zeros_like(x)
    for e in range(N_EXPERTS):
        hid = jax.nn.silu(x @ w_gate[e]) * (x @ w_up[e])
        y = y + combine[:, e:e + 1] * (hid @ w_down[e])
    return y.reshape(shp)


def _post_mix_channel_ple(h, mix, p, w_o, ln1_g, ln1_b, w_rg, b_rg, w_re, b_re, w_gate, w_up, w_down,
                          ln2_g, ln2_b, w_pg, b_pg, w_pp):
    h = _layer_norm(DEEPNORM_ALPHA * h + mix @ w_o, ln1_g, ln1_b)
    h = _layer_norm(DEEPNORM_ALPHA * h + _hier_moe(h, w_rg, b_rg, w_re, b_re, w_gate, w_up, w_down), ln2_g, ln2_b)
    gate = jax.nn.sigmoid(h @ w_pg + b_pg)
    return h + gate * (p @ w_pp)


def setup_inputs(seed: int = 0) -> dict:
    key = jax.random.key(seed)
    ks = jax.random.split(key, 40)
    nrm = lambda k, shp, s: jax.random.normal(k, shp, jnp.float32) * s
    n_pages = PAST_LEN // PAGE_SIZE
    n_used = DEC_BATCH * n_pages
    n_phys = n_used + max(1, n_used // 4)
    perm = jax.random.permutation(ks[0], n_phys)
    page_table = perm[:n_used].reshape(DEC_BATCH, n_pages).astype(jnp.int32)
    D = D_MODEL
    return {
        'x_prompt': nrm(ks[1], (BATCH, SEQ, D), 1.0),
        'x_sample': nrm(ks[2], (DEC_BATCH, DEC_SEQ, D), 1.0),
        'cache_k': nrm(ks[3], (DEPTH, n_phys, PAGE_SIZE, N_DIFF_HEADS, 2 * HEAD_DIM), 1.0),
        'cache_v': nrm(ks[4], (DEPTH, n_phys, PAGE_SIZE, N_DIFF_HEADS, V_DIM), 1.0),
        'page_table': page_table,
        'p_prompt': nrm(ks[5], (DEPTH, BATCH, SEQ, PLE_DIM), 1.0),
        'p_sample': nrm(ks[6], (DEPTH, DEC_BATCH, DEC_SEQ, PLE_DIM), 1.0),
        'ln_emb_g': 1.0 + nrm(ks[7], (D,), 0.02),
        'ln_emb_b': nrm(ks[8], (D,), 0.02),
        'w_in': nrm(ks[9], (DEPTH, D, PROJ_COLS), D ** -0.5),
        'lambda_q1': nrm(ks[10], (DEPTH, HEAD_DIM), 0.1),
        'lambda_k1': nrm(ks[11], (DEPTH, HEAD_DIM), 0.1),
        'lambda_q2': nrm(ks[12], (DEPTH, HEAD_DIM), 0.1),
        'lambda_k2': nrm(ks[13], (DEPTH, HEAD_DIM), 0.1),
        'subln_g': 1.0 + nrm(ks[14], (DEPTH, V_DIM), 0.02),
        'rel_bias': nrm(ks[15], (N_BUCKETS, N_DIFF_HEADS), 0.5),
        'sgu_ln_g': 1.0 + nrm(ks[16], (DEPTH, SGU_WIDTH), 0.02),
        'sgu_ln_b': nrm(ks[17], (DEPTH, SGU_WIDTH), 0.02),
        'sgu_w': nrm(ks[18], (DEPTH, N_SGU_GROUPS, CHUNK, CHUNK), CHUNK ** -0.5),
        'sgu_b': 1.0 + nrm(ks[19], (DEPTH, N_SGU_GROUPS, CHUNK), 0.1),
        'w_o': nrm(ks[20], (DEPTH, MIX_WIDTH, D), MIX_WIDTH ** -0.5 * DEEPNORM_BETA),
        'ln1_g': 1.0 + nrm(ks[21], (DEPTH, D), 0.02),
        'ln1_b': nrm(ks[22], (DEPTH, D), 0.02),
        'w_router_group': nrm(ks[23], (DEPTH, D, N_EXPERT_GROUPS), D ** -0.5),
        'b_router_group': nrm(ks[24], (DEPTH, N_EXPERT_GROUPS), 0.01),
        'w_router_expert': nrm(ks[25], (DEPTH, D, N_EXPERTS), D ** -0.5),
        'b_router_expert': nrm(ks[26], (DEPTH, N_EXPERTS), 0.01),
        'w_gate': nrm(ks[27], (DEPTH, N_EXPERTS, D, D_EXPERT), D ** -0.5),
        'w_up': nrm(ks[28], (DEPTH, N_EXPERTS, D, D_EXPERT), D ** -0.5),
        'w_down': nrm(ks[29], (DEPTH, N_EXPERTS, D_EXPERT, D), D_EXPERT ** -0.5 * DEEPNORM_BETA),
        'ln2_g': 1.0 + nrm(ks[30], (DEPTH, D), 0.02),
        'ln2_b': nrm(ks[31], (DEPTH, D), 0.02),
        'w_ple_gate': nrm(ks[32], (DEPTH, D, D), D ** -0.5),
        'b_ple_gate': nrm(ks[33], (DEPTH, D), 0.01),
        'w_ple_proj': nrm(ks[34], (DEPTH, PLE_DIM, D), PLE_DIM ** -0.5 * DEEPNORM_BETA),
    }


def reference(x_prompt, x_sample, cache_k, cache_v, page_table, p_prompt, p_sample,
              ln_emb_g, ln_emb_b, w_in, lambda_q1, lambda_k1, lambda_q2, lambda_k2, subln_g, rel_bias,
              sgu_ln_g, sgu_ln_b, sgu_w, sgu_b, w_o, ln1_g, ln1_b,
              w_router_group, b_router_group, w_router_expert, b_router_expert,
              w_gate, w_up, w_down, ln2_g, ln2_b, w_ple_gate, b_ple_gate, w_ple_proj):
    hp = _layer_norm(x_prompt, ln_emb_g, ln_emb_b)
    hs = _layer_norm(x_sample, ln_emb_g, ln_emb_b)
    k_p_rows, v_p_rows, k_s_rows, v_s_rows, sgu_s_rows = [], [], [], [], []
    for l in range(DEPTH):
        lam_init = 0.8 - 0.6 * math.exp(-0.3 * l)
        lam = _diff_lambda(lambda_q1[l], lambda_k1[l], lambda_q2[l], lambda_k2[l], lam_init)
        qp, kp, vp, up, gp = _split_proj(hp @ w_in[l])
        qs, ks_, vs, us, gs = _split_proj(hs @ w_in[l])
        att_p = _diff_head_out(_diff_attn_prompt(qp, kp, vp, lam, rel_bias), subln_g[l], lam_init, hp.dtype)
        att_s = _diff_head_out(_diff_attn_sample(qs, ks_, vs, cache_k[l], cache_v[l], page_table, lam, rel_bias),
                               subln_g[l], lam_init, hs.dtype)
        sgu_p, _ = _sgu(up, gp, sgu_ln_g[l], sgu_ln_b[l], sgu_w[l], sgu_b[l])
        sgu_s, v_norm_s = _sgu(us, gs, sgu_ln_g[l], sgu_ln_b[l], sgu_w[l], sgu_b[l])
        hp = _post_mix_channel_ple(hp, jnp.concatenate([att_p, sgu_p], -1), p_prompt[l], w_o[l], ln1_g[l], ln1_b[l],
                                   w_router_group[l], b_router_group[l], w_router_expert[l], b_router_expert[l],
                                   w_gate[l], w_up[l], w_down[l], ln2_g[l], ln2_b[l],
                                   w_ple_gate[l], b_ple_gate[l], w_ple_proj[l])
        hs = _post_mix_channel_ple(hs, jnp.concatenate([att_s, sgu_s], -1), p_sample[l], w_o[l], ln1_g[l], ln1_b[l],
                                   w_router_group[l], b_router_group[l], w_router_expert[l], b_router_expert[l],
                                   w_gate[l], w_up[l], w_down[l], ln2_g[l], ln2_b[l],
                                   w_ple_gate[l], b_ple_gate[l], w_ple_proj[l])
        k_p_rows.append(kp)
        v_p_rows.append(vp)
        k_s_rows.append(ks_)
        v_s_rows.append(vs)
        sgu_s_rows.append(v_norm_s)
    return (hp, hs, jnp.stack(k_p_rows), jnp.stack(v_p_rows), jnp.stack(k_s_rows), jnp.stack(v_s_rows),
            jnp.stack(sgu_s_rows))
```

```python
import functools
import math

import numpy as np
import jax
import jax.numpy as jnp
from jax import lax
from jax.experimental import pallas as pl
from jax.experimental.pallas import tpu as pltpu

F32 = jnp.float32
BF16 = jnp.bfloat16
I32 = jnp.int32

LN_EPS = 1e-5
NEG_INF = -1e30
MAX_DISTANCE = 128
TOP_K = 2
LANES = 128
VMEM_LIMIT = 56 * 1024 * 1024


def _cparams(sem, vmem=VMEM_LIMIT):
    return pltpu.CompilerParams(dimension_semantics=sem, vmem_limit_bytes=vmem)


def _ln(x, g, b):
    mu = jnp.mean(x, axis=-1, keepdims=True)
    xc = x - mu
    var = jnp.mean(xc * xc, axis=-1, keepdims=True)
    return xc * lax.rsqrt(var + LN_EPS) * g + b


def _gelu_tanh(x):
    c = math.sqrt(2.0 / math.pi)
    return 0.5 * x * (1.0 + jnp.tanh(c * (x + 0.044715 * (x * x * x))))


def _sigmoid(x):
    return 1.0 / (1.0 + jnp.exp(-x))


def _resident(shape):
    nd = len(shape)
    return pl.BlockSpec(shape, lambda *_: (0,) * nd, pipeline_mode=pl.Buffered(1))


def _t5_bucket(dist, n_buckets):
    def run(ft):
        n = np.maximum(dist, 0)
        max_exact = n_buckets // 2
        nf = np.maximum(n, max_exact).astype(ft)
        large = max_exact + (np.log(nf / ft(max_exact)) / ft(math.log(MAX_DISTANCE / max_exact))
                             * ft(n_buckets - max_exact)).astype(np.int32)
        large = np.minimum(large, n_buckets - 1)
        return np.where(n < max_exact, n, large).astype(np.int32)
    b32, b64 = run(np.float32), run(np.float64)
    assert (b32 == b64).all(), "bucket boundaries are precision sensitive"
    return b32


def _const_bucket_distance(n_buckets):
    d = np.arange(0, 4 * MAX_DISTANCE)
    b = _t5_bucket(d, n_buckets)
    below = np.nonzero(b != n_buckets - 1)[0]
    return int(below.max()) + 1


def _bias_kernel(rb_ref, bkt_ref, out_ref, *, n_buckets):
    h = pl.program_id(0)
    bkt = bkt_ref[...]
    c_last = rb_ref[n_buckets - 1, h]
    acc = jnp.zeros(bkt.shape, F32)
    for b in range(n_buckets - 1):
        acc = jnp.where(bkt == b, rb_ref[b, h] - c_last, acc)
    out_ref[0] = acc


def _bias_tables(rel_bias, bucket_np):
    n_buckets, n_heads = rel_bias.shape
    r, c = bucket_np.shape
    return pl.pallas_call(
        functools.partial(_bias_kernel, n_buckets=n_buckets),
        grid=(n_heads,),
        in_specs=[pl.BlockSpec(memory_space=pltpu.SMEM),
                  pl.BlockSpec((r, c), lambda h: (0, 0))],
        out_specs=pl.BlockSpec((1, r, c), lambda h: (h, 0, 0)),
        out_shape=jax.ShapeDtypeStruct((n_heads, r, c), F32),
        compiler_params=_cparams(("arbitrary",)),
        name="bias_tables",
    )(rel_bias, jnp.asarray(bucket_np))


def _inproj_kernel(x_ref, lng_ref, lnb_ref, w_ref, sg_ref, sb_ref, wmix_ref, bmix_ref,
                   q_ref, k4_ref, v4_ref, kb_ref, vb_ref, s_ref, *vn_refs, chunk, dw, sw, n_groups, q_scale):
    xn = _ln(x_ref[...], lng_ref[...], lnb_ref[...])
    xb = xn.astype(BF16)

    def proj(lo, n):
        return jnp.dot(xb, w_ref[:, lo:lo + n], preferred_element_type=F32)

    q_ref[...] = (proj(0, dw) * q_scale).astype(q_ref.dtype)
    n_heads, vd = k4_ref.shape[1:]
    for lo, r4, rb in ((dw, k4_ref, kb_ref), (2 * dw, v4_ref, vb_ref)):
        val = proj(lo, dw)
        rb[...] = val.astype(BF16)
        for h in range(n_heads):
            r4[:, h, :] = val[:, h * vd:(h + 1) * vd]
    u = _gelu_tanh(proj(3 * dw, sw))
    vn = _ln(_gelu_tanh(proj(3 * dw + sw, sw)), sg_ref[...], sb_ref[...])
    if vn_refs:
        vn_refs[0][...] = vn
    vb = vn.astype(BF16)
    tm = xb.shape[0]
    shift = chunk.bit_length() - 1
    row = lax.broadcasted_iota(I32, (tm, tm), 0)
    col = lax.broadcasted_iota(I32, (tm, tm), 1)
    causal = ((row >> shift) == (col >> shift)) & (col <= row)
    gw = sw // n_groups
    for g in range(n_groups):
        mg = jnp.where(causal, wmix_ref[g], 0.0).astype(BF16)
        sv = jnp.dot(mg, vb[:, g * gw:(g + 1) * gw], preferred_element_type=F32) + bmix_ref[:, g:g + 1]
        s_ref[:, g * gw:(g + 1) * gw] = (u[:, g * gw:(g + 1) * gw] * sv).astype(s_ref.dtype)


def _inproj(x2d, ln_g, ln_b, w_in_b, sgu_g, sgu_b_ln, sgu_w, sgu_bias, *, seq, tm, dw, sw, n_heads,
            q_scale, q_dtype, emit_vnorm):
    t, d = x2d.shape
    vd = dw // n_heads
    n_groups, chunk_full, _ = sgu_w.shape
    chunk = min(chunk_full, seq)
    assert chunk & (chunk - 1) == 0 and seq % chunk == 0 and tm % chunk == 0 and t % tm == 0
    rep = tm // chunk
    wmix = jnp.tile(sgu_w[:, :chunk, :chunk], (1, rep, rep))
    bmix = jnp.tile(sgu_bias[:, :chunk].T, (rep, 1))
    cols = w_in_b.shape[1]
    tok = lambda w: pl.BlockSpec((tm, w), lambda i: (i, 0))
    tok4 = pl.BlockSpec((tm, n_heads, vd), lambda i: (i, 0, 0))
    out_shape = [jax.ShapeDtypeStruct((t, dw), q_dtype), jax.ShapeDtypeStruct((t, n_heads, vd), F32),
                 jax.ShapeDtypeStruct((t, n_heads, vd), F32), jax.ShapeDtypeStruct((t, dw), BF16),
                 jax.ShapeDtypeStruct((t, dw), BF16), jax.ShapeDtypeStruct((t, sw), BF16)]
    out_specs = [tok(dw), tok4, tok4, tok(dw), tok(dw), tok(sw)]
    if emit_vnorm:
        out_shape.append(jax.ShapeDtypeStruct((t, sw), F32))
        out_specs.append(tok(sw))
    return pl.pallas_call(
        functools.partial(_inproj_kernel, chunk=chunk, dw=dw, sw=sw, n_groups=n_groups, q_scale=q_scale),
        grid=(t // tm,),
        in_specs=[tok(d), _resident((1, d)), _resident((1, d)), _resident((d, cols)),
                  _resident((1, sw)), _resident((1, sw)), _resident((n_groups, tm, tm)),
                  _resident((tm, n_groups))],
        out_specs=out_specs,
        out_shape=out_shape,
        compiler_params=_cparams(("arbitrary",)),
        name="inproj_sgu",
    )(x2d, ln_g.reshape(1, d), ln_b.reshape(1, d), w_in_b, sgu_g.reshape(1, sw), sgu_b_ln.reshape(1, sw),
      wmix, bmix)


def _diff_lambda_in_kernel(lam_ref, lam_init):
    lv = lam_ref[...]
    a = jnp.sum(lv[0:1] * lv[1:2], axis=-1, keepdims=True)
    b = jnp.sum(lv[2:3] * lv[3:4], axis=-1, keepdims=True)
    return jnp.exp(a) - jnp.exp(b) + lam_init


def _head_out(o, g, lam_init):
    ms = jnp.mean(o * o, axis=-1, keepdims=True)
    return o * lax.rsqrt(ms + LN_EPS) * g * (1.0 - lam_init)


def _online(s, m, l, acc, vblk):
    m_new = jnp.maximum(m, jnp.max(s, axis=-1, keepdims=True))
    corr = jnp.exp(m - m_new)
    p = jnp.exp(s - m_new)
    l_new = l * corr + jnp.sum(p, axis=-1, keepdims=True)
    acc_new = acc * corr + jnp.dot(p.astype(BF16), vblk, preferred_element_type=F32)
    return m_new, l_new, acc_new


def _attn_prompt_kernel(q_ref, k_ref, v_ref, bias_ref, lam_ref, g_ref, o_ref, kt_sc, *, tq, hd, lam_init):
    qi = pl.program_id(2)
    nk = kt_sc.shape[0]

    @pl.when(qi == 0)
    def _():
        for c in range(nk):
            kt_sc[c] = k_ref[0, c * tq:(c + 1) * tq, :].astype(F32).T.astype(BF16)

    q = q_ref[0]
    vd = q.shape[-1]
    lane = lax.broadcasted_iota(I32, (tq, vd), 1)
    zero = jnp.zeros_like(q)
    q1 = jnp.where(lane < hd, q, zero)
    q2 = jnp.where(lane >= hd, q, zero)

    def block(j, carry, bias, mask):
        m1, l1, a1, m2, l2, a2 = carry
        kblk = kt_sc[j]
        vblk = v_ref[0, pl.ds(pl.multiple_of(j * tq, tq), tq), :]
        s1 = jnp.dot(q1, kblk, preferred_element_type=F32)
        s2 = jnp.dot(q2, kblk, preferred_element_type=F32)
        if bias is not None:
            s1 = s1 + bias
            s2 = s2 + bias
        if mask is not None:
            s1 = jnp.where(mask, s1, NEG_INF)
            s2 = jnp.where(mask, s2, NEG_INF)
        m1, l1, a1 = _online(s1, m1, l1, a1, vblk)
        m2, l2, a2 = _online(s2, m2, l2, a2, vblk)
        return m1, l1, a1, m2, l2, a2

    init = (jnp.full((tq, 1), NEG_INF, F32), jnp.zeros((tq, 1), F32), jnp.zeros((tq, vd), F32),
            jnp.full((tq, 1), NEG_INF, F32), jnp.zeros((tq, 1), F32), jnp.zeros((tq, vd), F32))
    n_far = jnp.maximum(qi - 1, 0)
    carry = lax.fori_loop(0, n_far, lambda j, c: block(j, c, None, None), init)
    carry = lax.fori_loop(n_far, qi, lambda j, c: block(j, c, bias_ref[0, tq:2 * tq, :], None), carry)
    row = lax.broadcasted_iota(I32, (tq, tq), 0)
    col = lax.broadcasted_iota(I32, (tq, tq), 1)
    m1, l1, a1, m2, l2, a2 = block(qi, carry, bias_ref[0, 0:tq, :], col <= row)
    lam = _diff_lambda_in_kernel(lam_ref, lam_init)
    o = a1 / l1 - lam * (a2 / l2)
    o_ref[0] = _head_out(o, g_ref[...], lam_init).astype(o_ref.dtype)


def _attn_prompt(q, k, v, bias_p, lam_vecs, subln_g, *, n_heads, tq, lam_init):
    b, s, dw = q.shape
    vd = dw // n_heads
    hd = vd // 2
    nq = s // tq
    return pl.pallas_call(
        functools.partial(_attn_prompt_kernel, tq=tq, hd=hd, lam_init=lam_init),
        grid=(b, n_heads, nq),
        in_specs=[pl.BlockSpec((1, tq, vd), lambda bi, h, qi: (bi, qi, h)),
                  pl.BlockSpec((1, s, vd), lambda bi, h, qi: (bi, 0, h)),
                  pl.BlockSpec((1, s, vd), lambda bi, h, qi: (bi, 0, h)),
                  pl.BlockSpec((1, 2 * tq, tq), lambda bi, h, qi: (h, 0, 0)),
                  pl.BlockSpec((4, hd), lambda bi, h, qi: (0, 0)),
                  pl.BlockSpec((1, vd), lambda bi, h, qi: (0, 0))],
        out_specs=pl.BlockSpec((1, tq, vd), lambda bi, h, qi: (bi, qi, h)),
        out_shape=jax.ShapeDtypeStruct((b, s, dw), BF16),
        scratch_shapes=[pltpu.VMEM((nq, vd, tq), BF16)],
        compiler_params=_cparams(("arbitrary", "arbitrary", "arbitrary")),
        name="attn_prompt",
    )(q, k, v, bias_p, lam_vecs, subln_g.reshape(1, vd))


def _attn_decode_kernel(pt_ref, *refs, n_page_refs, n_heads, tq, hd, page, lam_init):
    k_refs = refs[:n_page_refs]
    v_refs = refs[n_page_refs:2 * n_page_refs]
    (q_ref, kn_ref, vn_ref, bias_ref, lam_ref, g_ref, o_ref,
     qz_sc, m_sc, l_sc, acc_sc) = refs[2 * n_page_refs:]
    del pt_ref
    step = pl.program_id(1)
    last = pl.num_programs(1) - 1
    vd = 2 * hd
    r2 = 2 * tq

    @pl.when(step == 0)
    def _():
        lane = lax.broadcasted_iota(I32, (tq, vd), 1)
        for h in range(n_heads):
            qh = q_ref[:, h * vd:(h + 1) * vd].astype(F32)
            qz_sc[h] = jnp.concatenate([jnp.where(lane < hd, qh, 0.0), jnp.where(lane >= hd, qh, 0.0)],
                                       axis=0).astype(BF16)
        m_sc[...] = jnp.full(m_sc.shape, NEG_INF, F32)
        l_sc[...] = jnp.zeros(l_sc.shape, F32)
        acc_sc[...] = jnp.zeros(acc_sc.shape, F32)

    def update(h, kh, vh, bias, mask):
        s = lax.dot_general(qz_sc[h], kh.astype(BF16), (((1,), (1,)), ((), ())), preferred_element_type=F32)
        if bias is not None:
            s = s + bias
        if mask is not None:
            s = jnp.where(mask, s, NEG_INF)
        m, l, acc = _online(s, m_sc[h], l_sc[h], acc_sc[h], vh.astype(BF16))
        m_sc[h] = m
        l_sc[h] = l
        acc_sc[h] = acc

    is_last = (step == last).astype(F32)
    for i in range(n_page_refs):
        for h in range(n_heads):
            bias = bias_ref[h, 0:r2, :] * is_last if i == n_page_refs - 1 else None
            update(h, k_refs[i][0, :, h, :], v_refs[i][0, :, h, :], bias, None)

    @pl.when(step == last)
    def _():
        qrow = lax.broadcasted_iota(I32, (r2, page), 0)
        qrow = jnp.where(qrow >= tq, qrow - tq, qrow)
        kcol = lax.broadcasted_iota(I32, (r2, page), 1)
        mask = kcol <= qrow
        pad = jnp.zeros((page - tq, vd), F32)
        lam = _diff_lambda_in_kernel(lam_ref, lam_init)
        for h in range(n_heads):
            kh = jnp.concatenate([kn_ref[:, h, :], pad], axis=0)
            vh = jnp.concatenate([vn_ref[:, h, :], pad], axis=0)
            update(h, kh, vh, bias_ref[h, r2:2 * r2, :], mask)
            acc = acc_sc[h]
            l = l_sc[h]
            o = acc[0:tq] / l[0:tq] - lam * (acc[tq:r2] / l[tq:r2])
            o_ref[:, h * vd:(h + 1) * vd] = _head_out(o, g_ref[...], lam_init).astype(o_ref.dtype)


def _attn_decode(q, k_new, v_new, cache_k2, cache_v2, page_table, bias_d, lam_vecs, subln_g,
                 *, n_heads, tq, pages_per_step, lam_init):
    t, dw = q.shape
    bd, n_pages = page_table.shape
    _, page, _, vd = cache_k2.shape
    hd = vd // 2
    pps = pages_per_step
    assert n_pages % pps == 0 and tq % 8 == 0 and tq <= page
    n_steps = n_pages // pps

    def page_spec(i):
        return pl.BlockSpec((1, page, n_heads, vd), lambda b, s, pt: (pt[b * n_pages + s * pps + i], 0, 0, 0))

    tokb = pl.BlockSpec((tq, dw), lambda b, s, pt: (b, 0))
    tok4 = pl.BlockSpec((tq, n_heads, vd), lambda b, s, pt: (b, 0, 0))
    full = lambda shape: pl.BlockSpec(shape, lambda b, s, pt: (0,) * len(shape))
    grid_spec = pltpu.PrefetchScalarGridSpec(
        num_scalar_prefetch=1,
        grid=(bd, n_steps),
        in_specs=[page_spec(i) for i in range(pps)] + [page_spec(i) for i in range(pps)]
        + [tokb, tok4, tok4, full(bias_d.shape), full((4, hd)), full((1, vd))],
        out_specs=tokb,
        scratch_shapes=[pltpu.VMEM((n_heads, 2 * tq, vd), BF16), pltpu.VMEM((n_heads, 2 * tq, 1), F32),
                        pltpu.VMEM((n_heads, 2 * tq, 1), F32), pltpu.VMEM((n_heads, 2 * tq, vd), F32)],
    )
    return pl.pallas_call(
        functools.partial(_attn_decode_kernel, n_page_refs=pps, n_heads=n_heads, tq=tq, hd=hd, page=page,
                          lam_init=lam_init),
        grid_spec=grid_spec,
        out_shape=jax.ShapeDtypeStruct((t, dw), F32),
        compiler_params=_cparams(("arbitrary", "arbitrary")),
        name="attn_decode",
    )(page_table.reshape(-1), *([cache_k2] * pps), *([cache_v2] * pps), q, k_new, v_new, bias_d, lam_vecs,
      subln_g.reshape(1, vd))


def _outproj_kernel(x_ref, att_ref, sgu_ref, lng_ref, lnb_ref, wo_ref, l1g_ref, l1b_ref, wr_ref, br_ref,
                    h1_ref, eidx_ref, topw_ref, cnt_ref, *, alpha, dw, n_experts, n_groups):
    i = pl.program_id(0)
    hp = _ln(x_ref[...], lng_ref[...], lnb_ref[...])
    mix = (jnp.dot(att_ref[...].astype(BF16), wo_ref[0:dw, :], preferred_element_type=F32)
           + jnp.dot(sgu_ref[...].astype(BF16), wo_ref[dw:, :], preferred_element_type=F32))
    h1 = _ln(alpha * hp + mix, l1g_ref[...], l1b_ref[...])
    h1_ref[...] = h1
    logits = jnp.dot(h1.astype(BF16), wr_ref[...], preferred_element_type=F32) + br_ref[...]
    tm = logits.shape[0]
    epg = n_experts // n_groups
    lane = lax.broadcasted_iota(I32, (tm, LANES), 1).astype(F32)
    big = float(LANES)
    ninf = -jnp.inf
    is_g = (lane >= n_experts) & (lane < n_experts + n_groups)
    gmax = jnp.max(jnp.where(is_g, logits, ninf), axis=-1, keepdims=True)
    gidx = jnp.min(jnp.where(is_g & (logits == gmax), lane, big), axis=-1, keepdims=True) - n_experts
    gsum = jnp.sum(jnp.where(is_g, jnp.exp(logits - gmax), 0.0), axis=-1, keepdims=True)
    g_w = 1.0 / gsum
    sel = (lane >= gidx * epg) & (lane < (gidx + 1.0) * epg)
    v1 = jnp.max(jnp.where(sel, logits, ninf), axis=-1, keepdims=True)
    i1 = jnp.min(jnp.where(sel & (logits == v1), lane, big), axis=-1, keepdims=True)
    sel2 = sel & (lane != i1)
    v2 = jnp.max(jnp.where(sel2, logits, ninf), axis=-1, keepdims=True)
    i2 = jnp.min(jnp.where(sel2 & (logits == v2), lane, big), axis=-1, keepdims=True)
    e = jnp.exp(v2 - v1)
    eidx_ref[:, 0:1] = i1.astype(I32)
    eidx_ref[:, 1:2] = i2.astype(I32)
    topw_ref[:, 0:1] = g_w / (1.0 + e)
    topw_ref[:, 1:2] = g_w * e / (1.0 + e)
    onehot = ((lane == i1) | (lane == i2)).astype(F32)
    cnt = jnp.sum(onehot, axis=0, keepdims=True)

    @pl.when(i == 0)
    def _():
        cnt_ref[...] = jnp.zeros(cnt_ref.shape, F32)

    cnt_ref[...] += jnp.broadcast_to(cnt, cnt_ref.shape)


def _outproj_router(x2d, att, sgu, ln_g, ln_b, w_o_b, l1g, l1b, w_r_b, b_r, *, tm, alpha, n_experts, n_groups):
    t, d = x2d.shape
    dw = att.shape[1]
    sw = sgu.shape[1]
    tok = lambda w: pl.BlockSpec((tm, w), lambda i: (i, 0))
    return pl.pallas_call(
        functools.partial(_outproj_kernel, alpha=alpha, dw=dw, n_experts=n_experts, n_groups=n_groups),
        grid=(t // tm,),
        in_specs=[tok(d), tok(dw), tok(sw), _resident((1, d)), _resident((1, d)), _resident((dw + sw, d)),
                  _resident((1, d)), _resident((1, d)), _resident((d, LANES)), _resident((1, LANES))],
        out_specs=[tok(d), tok(TOP_K), tok(TOP_K), pl.BlockSpec((8, LANES), lambda i: (0, 0))],
        out_shape=[jax.ShapeDtypeStruct((t, d), F32), jax.ShapeDtypeStruct((t, TOP_K), I32),
                   jax.ShapeDtypeStruct((t, TOP_K), F32), jax.ShapeDtypeStruct((8, LANES), F32)],
        compiler_params=_cparams(("arbitrary",)),
        name="outproj_router",
    )(x2d, att, sgu, ln_g.reshape(1, d), ln_b.reshape(1, d), w_o_b, l1g.reshape(1, d), l1b.reshape(1, d),
      w_r_b, b_r)


def _pos_kernel(eidx_ref, base_ref, pos_ref, run_sc):
    i = pl.program_id(0)

    @pl.when(i == 0)
    def _():
        run_sc[...] = jnp.zeros(run_sc.shape, F32)

    tm = eidx_ref.shape[0]
    lane = lax.broadcasted_iota(I32, (tm, LANES), 1)
    oh0 = (lane == eidx_ref[:, 0:1]).astype(F32)
    oh1 = (lane == eidx_ref[:, 1:2]).astype(F32)
    oh = oh0 + oh1
    row = lax.broadcasted_iota(I32, (tm, tm), 0)
    col = lax.broadcasted_iota(I32, (tm, tm), 1)
    lower = (col < row).astype(BF16)
    rank = jnp.dot(lower, oh.astype(BF16), preferred_element_type=F32)
    posmat = rank + base_ref[...] + run_sc[...]
    for k, ohk in enumerate((oh0, oh1)):
        pk = jnp.sum(ohk * posmat, axis=-1, keepdims=True)
        pk_rows = jnp.broadcast_to(pk, (tm, LANES)).T
        pos_ref[0, k:k + 1, :] = pk_rows[0:1, :].astype(I32)
    run_sc[...] += jnp.sum(oh, axis=0, keepdims=True)


def _positions(eidx, base, *, tm):
    t = eidx.shape[0]
    nt = t // tm
    return pl.pallas_call(
        _pos_kernel,
        grid=(nt,),
        in_specs=[pl.BlockSpec((tm, TOP_K), lambda i: (i, 0)), pl.BlockSpec((1, LANES), lambda i: (0, 0))],
        out_specs=pl.BlockSpec((1, TOP_K, tm), lambda i: (i, 0, 0)),
        out_shape=jax.ShapeDtypeStruct((nt, TOP_K, tm), I32),
        scratch_shapes=[pltpu.VMEM((1, LANES), F32)],
        compiler_params=_cparams(("arbitrary",)),
        name="route_positions",
    )(eidx, base)


def _scatter_kernel(pos_ref, h1_hbm, xs_hbm, sem, *, tm):
    i = pl.program_id(0)

    def row_copy(t, k):
        return pltpu.make_async_copy(h1_hbm.at[pl.ds(i * tm + t, 1)], xs_hbm.at[pl.ds(pos_ref[0, k, t], 1)], sem)

    def issue(t, c):
        for k in range(TOP_K):
            row_copy(t, k).start()
        return c

    def drain(t, c):
        for k in range(TOP_K):
            row_copy(t, k).wait()
        return c

    lax.fori_loop(0, tm, issue, 0)
    lax.fori_loop(0, tm, drain, 0)


def _scatter_rows(h1, pos, *, tm):
    t, d = h1.shape
    nt = t // tm
    return pl.pallas_call(
        functools.partial(_scatter_kernel, tm=tm),
        grid=(nt,),
        in_specs=[pl.BlockSpec((1, TOP_K, tm), lambda i: (i, 0, 0), memory_space=pltpu.SMEM),
                  pl.BlockSpec(memory_space=pl.ANY)],
        out_specs=pl.BlockSpec(memory_space=pl.ANY),
        out_shape=jax.ShapeDtypeStruct((TOP_K * t, d), h1.dtype),
        scratch_shapes=[pltpu.SemaphoreType.DMA(())],
        compiler_params=_cparams(("arbitrary",)),
        name="scatter_to_experts",
    )(pos, h1)


def _expert_kernel(tile_ref, grp_ref, lo_ref, hi_ref, first_ref, x_ref, wg_ref, wu_ref, wd_ref, o_ref,
                   wgb, wub, wdb, *, tm):
    w = pl.program_id(0)
    changed = (w == 0) | (grp_ref[w] != grp_ref[jnp.maximum(w - 1, 0)])

    @pl.when(changed)
    def _():
        wgb[...] = wg_ref[0].astype(BF16)
        wub[...] = wu_ref[0].astype(BF16)
        wdb[...] = wd_ref[0].astype(BF16)

    xb = x_ref[...].astype(BF16)
    g = jnp.dot(xb, wgb[...], preferred_element_type=F32)
    u = jnp.dot(xb, wub[...], preferred_element_type=F32)
    hid = (g * _sigmoid(g) * u).astype(BF16)
    o = jnp.dot(hid, wdb[...], preferred_element_type=F32)
    row = tile_ref[w] * tm + lax.broadcasted_iota(I32, (tm, 1), 0)
    o = jnp.where((row >= lo_ref[w]) & (row < hi_ref[w]), o, 0.0)

    @pl.when(first_ref[w] == 1)
    def _():
        o_ref[...] = o

    @pl.when(first_ref[w] == 0)
    def _():
        o_ref[...] += o


def _expert_ffn(xs, w_gate, w_up, w_down, meta, *, tm):
    a, d = xs.shape
    n_experts, _, de = w_gate.shape
    tile_ids, grp_ids, lo, hi, first = meta
    n_work = tile_ids.shape[0]
    grid_spec = pltpu.PrefetchScalarGridSpec(
        num_scalar_prefetch=5,
        grid=(n_work,),
        in_specs=[pl.BlockSpec((tm, d), lambda w, ti, gi, lo_, hi_, fi: (ti[w], 0)),
                  pl.BlockSpec((1, d, de), lambda w, ti, gi, lo_, hi_, fi: (gi[w], 0, 0)),
                  pl.BlockSpec((1, d, de), lambda w, ti, gi, lo_, hi_, fi: (gi[w], 0, 0)),
                  pl.BlockSpec((1, de, d), lambda w, ti, gi, lo_, hi_, fi: (gi[w], 0, 0))],
        out_specs=pl.BlockSpec((tm, d), lambda w, ti, gi, lo_, hi_, fi: (ti[w], 0)),
        scratch_shapes=[pltpu.VMEM((d, de), BF16), pltpu.VMEM((d, de), BF16), pltpu.VMEM((de, d), BF16)],
    )
    return pl.pallas_call(
        functools.partial(_expert_kernel, tm=tm),
        grid_spec=grid_spec,
        out_shape=jax.ShapeDtypeStruct((a, d), F32),
        compiler_params=_cparams(("arbitrary",)),
        name="expert_ffn",
    )(tile_ids, grp_ids, lo, hi, first, xs, w_gate, w_up, w_down)


def _group_metadata(counts, *, n_rows, tm):
    n_experts = counts.shape[0]
    nt = n_rows // tm
    n_work = nt + n_experts - 1
    ends = jnp.cumsum(counts)
    starts = ends - counts
    ntiles_g = jnp.where(counts > 0, (ends - 1) // tm - starts // tm + 1, 0)
    work_end = jnp.cumsum(ntiles_g)
    work_start = work_end - ntiles_g
    total = work_end[-1]
    w = jnp.arange(n_work, dtype=I32)
    wc = jnp.minimum(w, total - 1)
    g = jnp.minimum(jnp.sum((work_end[None, :] <= wc[:, None]).astype(I32), axis=1), n_experts - 1)
    onehot = (g[:, None] == jnp.arange(n_experts, dtype=I32)[None, :]).astype(I32)
    pick = lambda a: jnp.sum(onehot * a[None, :], axis=1)
    starts_g, ends_g = pick(starts), pick(ends)
    tile = (starts_g // tm + (wc - pick(work_start))).astype(I32)
    valid = w < total
    lo = jnp.where(valid, jnp.maximum(starts_g, tile * tm), 0).astype(I32)
    hi = jnp.where(valid, jnp.minimum(ends_g, (tile + 1) * tm), 0).astype(I32)
    prev_tile = jnp.concatenate([jnp.full((1,), -1, I32), tile[:-1]])
    first = (tile != prev_tile).astype(I32)
    return tile, g, lo, hi, first


def _final_kernel(pos_ref, posn_ref, h1_ref, topw_ref, p_ref, os_hbm, l2g_ref, l2b_ref, wpg_ref, bpg_ref,
                  wpp_ref, y_ref, buf, sem, *, tm, alpha):
    i = pl.program_id(0)
    n = pl.num_programs(0)
    slot = i % 2

    def row_copy(pref, sl, t, k):
        return pltpu.make_async_copy(os_hbm.at[pl.ds(pref[0, k, t], 1)], buf.at[sl, k, pl.ds(t, 1)], sem.at[sl])

    def issue(pref, sl):
        def body(t, c):
            for k in range(TOP_K):
                row_copy(pref, sl, t, k).start()
            return c
        lax.fori_loop(0, tm, body, 0)

    @pl.when(i == 0)
    def _():
        issue(pos_ref, 0)

    @pl.when(i + 1 < n)
    def _():
        issue(posn_ref, 1 - slot)

    def drain(t, c):
        for k in range(TOP_K):
            row_copy(pos_ref, slot, t, k).wait()
        return c

    lax.fori_loop(0, tm, drain, 0)
    y = topw_ref[:, 0:1] * buf[slot, 0] + topw_ref[:, 1:2] * buf[slot, 1]
    h2 = _ln(alpha * h1_ref[...] + y, l2g_ref[...], l2b_ref[...])
    gate = _sigmoid(jnp.dot(h2.astype(BF16), wpg_ref[...], preferred_element_type=F32) + bpg_ref[...])
    y_ref[...] = h2 + gate * jnp.dot(p_ref[...].astype(BF16), wpp_ref[...], preferred_element_type=F32)


def _combine_ple(h1, topw, pos, p2d, out_sorted, l2g, l2b, w_pg_b, b_pg, w_pp_b, *, tm, alpha):
    t, d = h1.shape
    nt = t // tm
    ple = p2d.shape[1]
    tok = lambda w: pl.BlockSpec((tm, w), lambda i: (i, 0))
    return pl.pallas_call(
        functools.partial(_final_kernel, tm=tm, alpha=alpha),
        grid=(nt,),
        in_specs=[pl.BlockSpec((1, TOP_K, tm), lambda i: (i, 0, 0), memory_space=pltpu.SMEM),
                  pl.BlockSpec((1, TOP_K, tm), lambda i: (jnp.minimum(i + 1, nt - 1), 0, 0),
                               memory_space=pltpu.SMEM),
                  tok(d), tok(TOP_K), tok(ple), pl.BlockSpec(memory_space=pl.ANY),
                  _resident((1, d)), _resident((1, d)), _resident((d, d)), _resident((1, d)),
                  _resident((ple, d))],
        out_specs=tok(d),
        out_shape=jax.ShapeDtypeStruct((t, d), F32),
        scratch_shapes=[pltpu.VMEM((2, TOP_K, tm, d), F32), pltpu.SemaphoreType.DMA((2,))],
        compiler_params=_cparams(("arbitrary",)),
        name="combine_ple",
    )(pos, pos, h1, topw, p2d, out_sorted, l2g.reshape(1, d), l2b.reshape(1, d), w_pg_b, b_pg.reshape(1, d),
      w_pp_b)


def _post_mix(x2d, att, sgu, p2d, prm, *, tm, alpha, n_experts, n_groups):
    h1, eidx, topw, cnt = _outproj_router(
        x2d, att, sgu, prm["ln_emb_g"], prm["ln_emb_b"], prm["w_o"], prm["ln1_g"], prm["ln1_b"],
        prm["w_r"], prm["b_r"], tm=tm, alpha=alpha, n_experts=n_experts, n_groups=n_groups)
    t = x2d.shape[0]
    counts = cnt[0, :n_experts].astype(I32)
    starts = jnp.cumsum(counts) - counts
    base = jnp.pad(starts.astype(F32), (0, LANES - n_experts)).reshape(1, LANES)
    pos = _positions(eidx, base, tm=tm)
    xs = _scatter_rows(h1, pos, tm=tm)
    meta = _group_metadata(counts, n_rows=TOP_K * t, tm=tm)
    out_sorted = _expert_ffn(xs, prm["w_gate"], prm["w_up"], prm["w_down"], meta, tm=tm)
    return _combine_ple(h1, topw, pos, p2d, out_sorted, prm["ln2_g"], prm["ln2_b"], prm["w_pg"], prm["b_pg"],
                        prm["w_pp"], tm=tm, alpha=alpha)


def kernel(x_prompt, x_sample, cache_k, cache_v, page_table, p_prompt, p_sample, ln_emb_g, ln_emb_b, w_in, lambda_q1, lambda_k1, lambda_q2, lambda_k2, subln_g, rel_bias, sgu_ln_g, sgu_ln_b, sgu_w, sgu_b, w_o, ln1_g, ln1_b, w_router_group, b_router_group, w_router_expert, b_router_expert, w_gate, w_up, w_down, ln2_g, ln2_b, w_ple_gate, b_ple_gate, w_ple_proj):
    bsz, seq, d = x_prompt.shape
    bd, tq_dec, _ = x_sample.shape
    depth, _, page, n_heads, vd = cache_v.shape
    hd = vd // 2
    dw = n_heads * vd
    sw = (w_in.shape[-1] - 3 * dw) // 2
    n_groups = w_router_group.shape[-1]
    n_experts = w_router_expert.shape[-1]
    n_buckets = rel_bias.shape[0]
    ple = p_prompt.shape[-1]
    alpha = (2.0 * depth) ** 0.25
    assert n_experts + n_groups <= LANES

    tp, ts = bsz * seq, bd * tq_dec
    tm_p = min(256, tp)
    tm_s = min(256, ts)
    tq = min(256, seq)
    n_pages = page_table.shape[1]
    pps = math.gcd(n_pages, 4)

    d_const = _const_bucket_distance(n_buckets)
    assert tq + 1 >= d_const and page + 1 >= d_const
    qi = np.arange(tq)[:, None]
    kj = np.arange(tq)[None, :]
    bkt_p = np.concatenate([_t5_bucket(qi - kj, n_buckets), _t5_bucket(tq + qi - kj, n_buckets)], axis=0)
    di = np.tile(np.arange(tq_dec), 2)[:, None]
    dj = np.arange(page)[None, :]
    bkt_d = np.concatenate([_t5_bucket(page + di - dj, n_buckets), _t5_bucket(di - dj, n_buckets)], axis=0)
    bias_p = _bias_tables(rel_bias, bkt_p)
    bias_d = _bias_tables(rel_bias, bkt_d)

    xp = x_prompt.reshape(tp, d)
    xs = x_sample.reshape(ts, d)
    hp_x, hs_x = xp, xs
    k_p_rows, v_p_rows, k_s_rows, v_s_rows, sgu_s_rows = [], [], [], [], []
    for l in range(depth):
        assert depth == 1, "the trunk input of deeper layers is the previous layer's output"
        lam_init = 0.8 - 0.6 * math.exp(-0.3 * l)
        lam_vecs = jnp.stack([lambda_q1[l], lambda_k1[l], lambda_q2[l], lambda_k2[l]])
        w_in_b = w_in[l].astype(BF16)
        pad_r = LANES - n_experts - n_groups
        prm = dict(
            ln_emb_g=ln_emb_g, ln_emb_b=ln_emb_b, w_o=w_o[l].astype(BF16), ln1_g=ln1_g[l], ln1_b=ln1_b[l],
            w_r=jnp.pad(jnp.concatenate([w_router_expert[l], w_router_group[l]], axis=1),
                        ((0, 0), (0, pad_r))).astype(BF16),
            b_r=jnp.pad(jnp.concatenate([b_router_expert[l], b_router_group[l]]), (0, pad_r)).reshape(1, LANES),
            w_gate=w_gate[l], w_up=w_up[l], w_down=w_down[l], ln2_g=ln2_g[l], ln2_b=ln2_b[l],
            w_pg=w_ple_gate[l].astype(BF16), b_pg=b_ple_gate[l], w_pp=w_ple_proj[l].astype(BF16))
        common = dict(dw=dw, sw=sw, n_heads=n_heads, q_scale=hd ** -0.5)
        qp, kp, vp, kbp, vbp, sgu_p = _inproj(
            hp_x, ln_emb_g, ln_emb_b, w_in_b, sgu_ln_g[l], sgu_ln_b[l], sgu_w[l], sgu_b[l],
            seq=seq, tm=tm_p, q_dtype=BF16, emit_vnorm=False, **common)
        qs, ks, vs, _, _, sgu_s, vn_s = _inproj(
            hs_x, ln_emb_g, ln_emb_b, w_in_b, sgu_ln_g[l], sgu_ln_b[l], sgu_w[l], sgu_b[l],
            seq=tq_dec, tm=tm_s, q_dtype=F32, emit_vnorm=True, **common)
        att_p = _attn_prompt(qp.reshape(bsz, seq, dw), kbp.reshape(bsz, seq, dw), vbp.reshape(bsz, seq, dw),
                             bias_p, lam_vecs, subln_g[l], n_heads=n_heads, tq=tq, lam_init=lam_init)
        ck = cache_k.reshape(cache_k.shape[1:]) if depth == 1 else cache_k[l]
        cv = cache_v.reshape(cache_v.shape[1:]) if depth == 1 else cache_v[l]
        att_s = _attn_decode(qs, ks, vs, ck, cv, page_table, bias_d, lam_vecs, subln_g[l], n_heads=n_heads,
                             tq=tq_dec, pages_per_step=pps, lam_init=lam_init)
        post = dict(alpha=alpha, n_experts=n_experts, n_groups=n_groups)
        hp_x = _post_mix(hp_x, att_p.reshape(tp, dw), sgu_p, p_prompt[l].reshape(tp, ple), prm, tm=tm_p, **post)
        hs_x = _post_mix(hs_x, att_s, sgu_s, p_sample[l].reshape(ts, ple), prm, tm=tm_s, **post)
        k_p_rows.append(kp.reshape(bsz, seq, n_heads, vd))
        v_p_rows.append(vp.reshape(bsz, seq, n_heads, vd))
        k_s_rows.append(ks.reshape(bd, tq_dec, n_heads, vd))
        v_s_rows.append(vs.reshape(bd, tq_dec, n_heads, vd))
        sgu_s_rows.append(vn_s.reshape(bd, tq_dec, sw))
    stack = lambda rows: rows[0][None] if len(rows) == 1 else jnp.stack(rows)
    return (hp_x.reshape(bsz, seq, d), hs_x.reshape(bd, tq_dec, d), stack(k_p_rows), stack(v_p_rows),
            stack(k_s_rows), stack(v_s_rows), stack(sgu_s_rows))
```

```python
import functools
import math

import numpy as np
import jax
import jax.numpy as jnp
from jax import lax
from jax.experimental import pallas as pl
from jax.experimental.pallas import tpu as pltpu

F32 = jnp.float32
BF16 = jnp.bfloat16
I32 = jnp.int32

LN_EPS = 1e-5
NEG_INF = -1e30
MAX_DISTANCE = 128
TOP_K = 2
LANES = 128
LOG2E = math.log2(math.e)
VMEM_LIMIT = 56 * 1024 * 1024


def _cparams(sem, vmem=VMEM_LIMIT):
    return pltpu.CompilerParams(dimension_semantics=sem, vmem_limit_bytes=vmem)


def _ln(x, g, b):
    mu = jnp.mean(x, axis=-1, keepdims=True)
    xc = x - mu
    var = jnp.mean(xc * xc, axis=-1, keepdims=True)
    return xc * lax.rsqrt(var + LN_EPS) * g + b


def _gelu_tanh(x):
    c = math.sqrt(2.0 / math.pi)
    return 0.5 * x * (1.0 + jnp.tanh(c * (x + 0.044715 * (x * x * x))))


def _sigmoid(x):
    return 1.0 / (1.0 + jnp.exp(-x))


def _resident(shape):
    nd = len(shape)
    return pl.BlockSpec(shape, lambda *_: (0,) * nd, pipeline_mode=pl.Buffered(1))


def _t5_bucket(dist, n_buckets):
    def run(ft):
        n = np.maximum(dist, 0)
        max_exact = n_buckets // 2
        nf = np.maximum(n, max_exact).astype(ft)
        large = max_exact + (np.log(nf / ft(max_exact)) / ft(math.log(MAX_DISTANCE / max_exact))
                             * ft(n_buckets - max_exact)).astype(np.int32)
        large = np.minimum(large, n_buckets - 1)
        return np.where(n < max_exact, n, large).astype(np.int32)
    b32, b64 = run(np.float32), run(np.float64)
    assert (b32 == b64).all(), "bucket boundaries are precision sensitive"
    return b32


def _const_bucket_distance(n_buckets):
    d = np.arange(0, 4 * MAX_DISTANCE)
    b = _t5_bucket(d, n_buckets)
    below = np.nonzero(b != n_buckets - 1)[0]
    return int(below.max()) + 1


def _bias_kernel(rb_ref, bkt_ref, out_ref, *, n_buckets, head_stride):
    h = pl.program_id(0)
    bkt = bkt_ref[...]
    c_last = rb_ref[n_buckets - 1, h]
    acc = jnp.zeros(bkt.shape, F32)
    for b in range(n_buckets - 1):
        acc = jnp.where(bkt == b, (rb_ref[b, h] - c_last) * LOG2E, acc)
    dead = bkt < 0
    if head_stride:
        col = lax.broadcasted_iota(I32, bkt.shape, 1)
        dead = dead | ((col & (head_stride - 1)) != h)
    out_ref[0] = jnp.where(dead, NEG_INF, acc)


def _bias_tables(rel_bias, bucket_np, head_stride=0):
    n_buckets, n_heads = rel_bias.shape
    r, c = bucket_np.shape
    assert head_stride & (head_stride - 1) == 0
    return pl.pallas_call(
        functools.partial(_bias_kernel, n_buckets=n_buckets, head_stride=head_stride),
        grid=(n_heads,),
        in_specs=[pl.BlockSpec(memory_space=pltpu.SMEM),
                  pl.BlockSpec((r, c), lambda h: (0, 0))],
        out_specs=pl.BlockSpec((1, r, c), lambda h: (h, 0, 0)),
        out_shape=jax.ShapeDtypeStruct((n_heads, r, c), F32),
        compiler_params=_cparams(("arbitrary",)),
        name="bias_tables",
    )(rel_bias, jnp.asarray(bucket_np))


def _inproj_kernel(x_ref, lng_ref, lnb_ref, w_ref, sg_ref, sb_ref, wmix_ref, bmix_ref,
                   q_ref, k4_ref, v4_ref, kb_ref, vb_ref, s_ref, *vn_refs, chunk, dw, sw, n_groups, q_scale):
    xn = _ln(x_ref[...], lng_ref[...], lnb_ref[...])
    xb = xn.astype(BF16)

    def proj(lo, n):
        return jnp.dot(xb, w_ref[:, lo:lo + n], preferred_element_type=F32)

    q_ref[...] = (proj(0, dw) * q_scale).astype(q_ref.dtype)
    n_heads, vd = k4_ref.shape[1:]
    for lo, r4, rb in ((dw, k4_ref, kb_ref), (2 * dw, v4_ref, vb_ref)):
        val = proj(lo, dw)
        rb[...] = val.astype(BF16)
        for h in range(n_heads):
            r4[:, h, :] = val[:, h * vd:(h + 1) * vd]
    u = _gelu_tanh(proj(3 * dw, sw))
    vn = _ln(_gelu_tanh(proj(3 * dw + sw, sw)), sg_ref[...], sb_ref[...])
    if vn_refs:
        vn_refs[0][...] = vn
    vb = vn.astype(BF16)
    tm = xb.shape[0]
    shift = chunk.bit_length() - 1
    row = lax.broadcasted_iota(I32, (tm, tm), 0)
    col = lax.broadcasted_iota(I32, (tm, tm), 1)
    causal = ((row >> shift) == (col >> shift)) & (col <= row)
    gw = sw // n_groups
    for g in range(n_groups):
        mg = jnp.where(causal, wmix_ref[g], 0.0).astype(BF16)
        sv = jnp.dot(mg, vb[:, g * gw:(g + 1) * gw], preferred_element_type=F32) + bmix_ref[:, g:g + 1]
        s_ref[:, g * gw:(g + 1) * gw] = (u[:, g * gw:(g + 1) * gw] * sv).astype(s_ref.dtype)


def _inproj(x2d, ln_g, ln_b, w_in_b, sgu_g, sgu_b_ln, sgu_w, sgu_bias, *, seq, tm, dw, sw, n_heads,
            q_scale, q_dtype, emit_vnorm):
    t, d = x2d.shape
    vd = dw // n_heads
    n_groups, chunk_full, _ = sgu_w.shape
    chunk = min(chunk_full, seq)
    assert chunk & (chunk - 1) == 0 and seq % chunk == 0 and tm % chunk == 0 and t % tm == 0
    rep = tm // chunk
    wmix = jnp.tile(sgu_w[:, :chunk, :chunk], (1, rep, rep))
    bmix = jnp.tile(sgu_bias[:, :chunk].T, (rep, 1))
    cols = w_in_b.shape[1]
    tok = lambda w: pl.BlockSpec((tm, w), lambda i: (i, 0))
    tok4 = pl.BlockSpec((tm, n_heads, vd), lambda i: (i, 0, 0))
    out_shape = [jax.ShapeDtypeStruct((t, dw), q_dtype), jax.ShapeDtypeStruct((t, n_heads, vd), F32),
                 jax.ShapeDtypeStruct((t, n_heads, vd), F32), jax.ShapeDtypeStruct((t, dw), BF16),
                 jax.ShapeDtypeStruct((t, dw), BF16), jax.ShapeDtypeStruct((t, sw), BF16)]
    out_specs = [tok(dw), tok4, tok4, tok(dw), tok(dw), tok(sw)]
    if emit_vnorm:
        out_shape.append(jax.ShapeDtypeStruct((t, sw), F32))
        out_specs.append(tok(sw))
    return pl.pallas_call(
        functools.partial(_inproj_kernel, chunk=chunk, dw=dw, sw=sw, n_groups=n_groups, q_scale=q_scale),
        grid=(t // tm,),
        in_specs=[tok(d), _resident((1, d)), _resident((1, d)), _resident((d, cols)),
                  _resident((1, sw)), _resident((1, sw)), _resident((n_groups, tm, tm)),
                  _resident((tm, n_groups))],
        out_specs=out_specs,
        out_shape=out_shape,
        compiler_params=_cparams(("arbitrary",)),
        name="inproj_sgu",
    )(x2d, ln_g.reshape(1, d), ln_b.reshape(1, d), w_in_b, sgu_g.reshape(1, sw), sgu_b_ln.reshape(1, sw),
      wmix, bmix)


def _diff_lambda_in_kernel(lam_ref, lam_init):
    lv = lam_ref[...]
    a = jnp.sum(lv[0:1] * lv[1:2], axis=-1, keepdims=True)
    b = jnp.sum(lv[2:3] * lv[3:4], axis=-1, keepdims=True)
    return jnp.exp(a) - jnp.exp(b) + lam_init


def _attn_prompt_kernel(q_ref, k_ref, v_ref, bias_ref, lam_ref, g_ref, o_ref, vt_sc, *, tq, hd, nh, lam_init):
    qi = pl.program_id(2)
    nk = vt_sc.shape[1]
    vd = 2 * hd
    heads = range(nh)
    hcols = lambda hh: slice(hh * vd, (hh + 1) * vd)

    @pl.when(qi == 0)
    def _():
        for hh in heads:
            for c in range(nk):
                vt_sc[hh, c] = v_ref[0, c * tq:(c + 1) * tq, hcols(hh)].astype(F32).T.astype(BF16)

    sub = lax.broadcasted_iota(I32, (vd, tq), 0)
    qq = []
    for hh in heads:
        qt = q_ref[0, :, hcols(hh)].astype(F32).T
        qq.append(jnp.concatenate([jnp.where(sub < hd, qt, 0.0), jnp.where(sub >= hd, qt, 0.0)],
                                  axis=1).astype(BF16))

    def scores(j):
        rows = pl.ds(pl.multiple_of(j * tq, tq), tq)
        return tuple(jnp.dot(k_ref[0, rows, hcols(hh)], qq[hh], preferred_element_type=F32) for hh in heads)

    def update(j, state, s_all, bias_lo, mask):
        out = []
        for hh in heads:
            m, l, acc = state[hh]
            s = s_all[hh]
            if bias_lo is not None:
                s = s + bias_ref[hh, bias_lo:bias_lo + tq, :]
            if mask is not None:
                s = jnp.where(mask, s, NEG_INF)
            m_new = jnp.maximum(m, jnp.max(s, axis=0, keepdims=True))
            corr = jnp.exp2(m - m_new)
            p = jnp.exp2(s - m_new)
            l = l * corr + jnp.sum(p, axis=0, keepdims=True)
            acc = acc * corr + jnp.dot(vt_sc[hh, j], p.astype(BF16), preferred_element_type=F32)
            out.append((m_new, l, acc))
        return tuple(out)

    def step(bias_lo):
        def body(j, carry):
            state, s_all = carry
            s_next = scores(j + 1)
            return update(j, state, s_all, bias_lo, None), s_next
        return body

    init = tuple((jnp.full((1, 2 * tq), NEG_INF, F32), jnp.zeros((1, 2 * tq), F32),
                  jnp.zeros((vd, 2 * tq), F32)) for _ in heads)
    n_far = jnp.maximum(qi - 1, 0)
    carry = lax.fori_loop(0, n_far, step(None), (init, scores(0)))
    state, s_all = lax.fori_loop(n_far, qi, step(tq), carry)
    krow = lax.broadcasted_iota(I32, (tq, 2 * tq), 0)
    qcol = lax.broadcasted_iota(I32, (tq, 2 * tq), 1)
    qcol = jnp.where(qcol >= tq, qcol - tq, qcol)
    state = update(qi, state, s_all, 0, krow <= qcol)
    lam = _diff_lambda_in_kernel(lam_ref, lam_init)
    for hh in heads:
        _, l, acc = state[hh]
        ot = acc[:, :tq] / l[:, :tq] - lam * (acc[:, tq:] / l[:, tq:])
        ms = jnp.mean(ot * ot, axis=0, keepdims=True)
        ot = ot * lax.rsqrt(ms + LN_EPS) * g_ref[...] * (1.0 - lam_init)
        o_ref[0, :, hcols(hh)] = ot.T.astype(o_ref.dtype)


def _attn_prompt(q, k, v, bias_p, lam_vecs, subln_g, *, n_heads, tq, nh, lam_init):
    b, s, dw = q.shape
    vd = dw // n_heads
    hd = vd // 2
    nq = s // tq
    assert n_heads % nh == 0
    return pl.pallas_call(
        functools.partial(_attn_prompt_kernel, tq=tq, hd=hd, nh=nh, lam_init=lam_init),
        grid=(b, n_heads // nh, nq),
        in_specs=[pl.BlockSpec((1, tq, nh * vd), lambda bi, h, qi: (bi, qi, h)),
                  pl.BlockSpec((1, s, nh * vd), lambda bi, h, qi: (bi, 0, h)),
                  pl.BlockSpec((1, s, nh * vd), lambda bi, h, qi: (bi, 0, h)),
                  pl.BlockSpec((nh, 2 * tq, 2 * tq), lambda bi, h, qi: (h, 0, 0)),
                  pl.BlockSpec((4, hd), lambda bi, h, qi: (0, 0)),
                  pl.BlockSpec((vd, 1), lambda bi, h, qi: (0, 0))],
        out_specs=pl.BlockSpec((1, tq, nh * vd), lambda bi, h, qi: (bi, qi, h)),
        out_shape=jax.ShapeDtypeStruct((b, s, dw), BF16),
        scratch_shapes=[pltpu.VMEM((nh, nq, vd, tq), BF16)],
        compiler_params=_cparams(("arbitrary", "arbitrary", "arbitrary")),
        name="attn_prompt",
    )(q, k, v, bias_p, lam_vecs, subln_g.reshape(vd, 1))


def _attn_decode_kernel(pt_ref, *refs, n_page_refs, n_heads, tq, hd, lam_init):
    k_refs = refs[:n_page_refs]
    v_refs = refs[n_page_refs:2 * n_page_refs]
    (q_ref, kn_ref, vn_ref, pen_ref, lam_ref, g_ref, o_ref, qall_sc, m_sc, l_sc, acc_sc) = refs[2 * n_page_refs:]
    del pt_ref
    step = pl.program_id(1)
    last = pl.num_programs(1) - 1
    vd = 2 * hd
    r2 = 2 * tq
    nt_dims = (((1,), (1,)), ((), ()))

    @pl.when(step == 0)
    def _():
        lane = lax.broadcasted_iota(I32, (tq, vd), 1)
        rows = []
        for h in range(n_heads):
            qh = q_ref[:, h * vd:(h + 1) * vd]
            rows += [jnp.where(lane < hd, qh, 0.0), jnp.where(lane >= hd, qh, 0.0)]
        qall_sc[...] = jnp.concatenate(rows, axis=0).astype(BF16)
        m_sc[...] = jnp.full(m_sc.shape, NEG_INF, F32)
        l_sc[...] = jnp.zeros(l_sc.shape, F32)
        acc_sc[...] = jnp.zeros(acc_sc.shape, F32)

    def update(carry, blocks):
        m, l, acc = carry
        ss = [lax.dot_general(qall_sc[...], k2.astype(BF16), nt_dims, preferred_element_type=F32) + pen
              for k2, _, pen in blocks]
        smax = functools.reduce(jnp.maximum, ss)
        m_new = jnp.maximum(m, jnp.max(smax, axis=-1, keepdims=True))
        corr = jnp.exp2(m - m_new)
        acc = acc * corr
        psum = None
        for s, (_, v2, _) in zip(ss, blocks):
            p = jnp.exp2(s - m_new)
            psum = p if psum is None else psum + p
            acc = acc + jnp.dot(p.astype(BF16), v2.astype(BF16), preferred_element_type=F32)
        l = l * corr + jnp.sum(psum, axis=-1, keepdims=True)
        return m_new, l, acc

    pens = [pen_ref[0]] * (n_page_refs - 1) + [pen_ref[jnp.where(step == last, 1, 0)]]
    carry = update((m_sc[...], l_sc[...], acc_sc[...]),
                   [(k_refs[i][0], v_refs[i][0], pens[i]) for i in range(n_page_refs)])
    m_sc[...], l_sc[...], acc_sc[...] = carry

    @pl.when(step == last)
    def _():
        n_new = kn_ref.shape[0]
        pad = jnp.zeros((LANES - n_new, vd), F32)
        k2 = jnp.concatenate([kn_ref[...], pad], axis=0)
        v2 = jnp.concatenate([vn_ref[...], pad], axis=0)
        _, l, acc = update(carry, [(k2, v2, pen_ref[2, :, 0:LANES])])
        lam = _diff_lambda_in_kernel(lam_ref, lam_init)
        for h in range(n_heads):
            r0 = h * r2
            o = acc[r0:r0 + tq] / l[r0:r0 + tq] - lam * (acc[r0 + tq:r0 + r2] / l[r0 + tq:r0 + r2])
            ms = jnp.mean(o * o, axis=-1, keepdims=True)
            o_ref[:, h * vd:(h + 1) * vd] = (o * lax.rsqrt(ms + LN_EPS) * g_ref[...] * (1.0 - lam_init)
                                             ).astype(o_ref.dtype)


def _attn_decode(q, k_new, v_new, cache_k2, cache_v2, page_table, pen, lam_vecs, subln_g,
                 *, n_heads, tq, pages_per_step, lam_init):
    t, dw = q.shape
    bd, n_pages = page_table.shape
    _, page_rows, vd = cache_k2.shape
    hd = vd // 2
    pps = pages_per_step
    n_new = tq * n_heads
    rows = n_heads * 2 * tq
    assert n_pages % pps == 0 and tq % 8 == 0 and n_new <= LANES <= page_rows
    n_steps = n_pages // pps

    def page_spec(i):
        return pl.BlockSpec((1, page_rows, vd), lambda b, s, pt: (pt[b * n_pages + s * pps + i], 0, 0))

    tokb = pl.BlockSpec((tq, dw), lambda b, s, pt: (b, 0))
    newb = pl.BlockSpec((n_new, vd), lambda b, s, pt: (b, 0))
    full = lambda shape: pl.BlockSpec(shape, lambda b, s, pt: (0,) * len(shape))
    grid_spec = pltpu.PrefetchScalarGridSpec(
        num_scalar_prefetch=1,
        grid=(bd, n_steps),
        in_specs=[page_spec(i) for i in range(pps)] + [page_spec(i) for i in range(pps)]
        + [tokb, newb, newb, full(pen.shape), full((4, hd)), full((1, vd))],
        out_specs=tokb,
        scratch_shapes=[pltpu.VMEM((rows, vd), BF16), pltpu.VMEM((rows, 1), F32),
                        pltpu.VMEM((rows, 1), F32), pltpu.VMEM((rows, vd), F32)],
    )
    return pl.pallas_call(
        functools.partial(_attn_decode_kernel, n_page_refs=pps, n_heads=n_heads, tq=tq, hd=hd, lam_init=lam_init),
        grid_spec=grid_spec,
        out_shape=jax.ShapeDtypeStruct((t, dw), F32),
        compiler_params=_cparams(("arbitrary", "arbitrary")),
        name="attn_decode",
    )(page_table.reshape(-1), *([cache_k2] * pps), *([cache_v2] * pps), q, k_new, v_new, pen, lam_vecs,
      subln_g.reshape(1, vd))


def _outproj_kernel(x_ref, att_ref, sgu_ref, lng_ref, lnb_ref, wo_ref, l1g_ref, l1b_ref, wr_ref, br_ref,
                    h1_ref, eidx_ref, topw_ref, cnt_ref, *, alpha, dw, n_experts, n_groups):
    i = pl.program_id(0)
    hp = _ln(x_ref[...], lng_ref[...], lnb_ref[...])
    mix = (jnp.dot(att_ref[...].astype(BF16), wo_ref[0:dw, :], preferred_element_type=F32)
           + jnp.dot(sgu_ref[...].astype(BF16), wo_ref[dw:, :], preferred_element_type=F32))
    h1 = _ln(alpha * hp + mix, l1g_ref[...], l1b_ref[...])
    h1_ref[...] = h1
    logits = jnp.dot(h1.astype(BF16), wr_ref[...], preferred_element_type=F32) + br_ref[...]
    tm = logits.shape[0]
    epg = n_experts // n_groups
    lane = lax.broadcasted_iota(I32, (tm, LANES), 1).astype(F32)
    big = float(LANES)
    ninf = -jnp.inf
    is_g = (lane >= n_experts) & (lane < n_experts + n_groups)
    gmax = jnp.max(jnp.where(is_g, logits, ninf), axis=-1, keepdims=True)
    gidx = jnp.min(jnp.where(is_g & (logits == gmax), lane, big), axis=-1, keepdims=True) - n_experts
    gsum = jnp.sum(jnp.where(is_g, jnp.exp(logits - gmax), 0.0), axis=-1, keepdims=True)
    g_w = 1.0 / gsum
    sel = (lane >= gidx * epg) & (lane < (gidx + 1.0) * epg)
    v1 = jnp.max(jnp.where(sel, logits, ninf), axis=-1, keepdims=True)
    i1 = jnp.min(jnp.where(sel & (logits == v1), lane, big), axis=-1, keepdims=True)
    sel2 = sel & (lane != i1)
    v2 = jnp.max(jnp.where(sel2, logits, ninf), axis=-1, keepdims=True)
    i2 = jnp.min(jnp.where(sel2 & (logits == v2), lane, big), axis=-1, keepdims=True)
    e = jnp.exp(v2 - v1)
    eidx_ref[:, 0:1] = i1.astype(I32)
    eidx_ref[:, 1:2] = i2.astype(I32)
    topw_ref[:, 0:1] = g_w / (1.0 + e)
    topw_ref[:, 1:2] = g_w * e / (1.0 + e)
    onehot = ((lane == i1) | (lane == i2)).astype(F32)
    cnt = jnp.sum(onehot, axis=0, keepdims=True)

    @pl.when(i == 0)
    def _():
        cnt_ref[...] = jnp.zeros(cnt_ref.shape, F32)

    cnt_ref[...] += jnp.broadcast_to(cnt, cnt_ref.shape)


def _outproj_router(x2d, att, sgu, ln_g, ln_b, w_o_b, l1g, l1b, w_r_b, b_r, *, tm, alpha, n_experts, n_groups):
    t, d = x2d.shape
    dw = att.shape[1]
    sw = sgu.shape[1]
    tok = lambda w: pl.BlockSpec((tm, w), lambda i: (i, 0))
    return pl.pallas_call(
        functools.partial(_outproj_kernel, alpha=alpha, dw=dw, n_experts=n_experts, n_groups=n_groups),
        grid=(t // tm,),
        in_specs=[tok(d), tok(dw), tok(sw), _resident((1, d)), _resident((1, d)), _resident((dw + sw, d)),
                  _resident((1, d)), _resident((1, d)), _resident((d, LANES)), _resident((1, LANES))],
        out_specs=[tok(d), tok(TOP_K), tok(TOP_K), pl.BlockSpec((8, LANES), lambda i: (0, 0))],
        out_shape=[jax.ShapeDtypeStruct((t, d), F32), jax.ShapeDtypeStruct((t, TOP_K), I32),
                   jax.ShapeDtypeStruct((t, TOP_K), F32), jax.ShapeDtypeStruct((8, LANES), F32)],
        compiler_params=_cparams(("arbitrary",)),
        name="outproj_router",
    )(x2d, att, sgu, ln_g.reshape(1, d), ln_b.reshape(1, d), w_o_b, l1g.reshape(1, d), l1b.reshape(1, d),
      w_r_b, b_r)


def _pos_kernel(eidx_ref, base_ref, pos_ref, run_sc):
    i = pl.program_id(0)

    @pl.when(i == 0)
    def _():
        run_sc[...] = jnp.zeros(run_sc.shape, F32)

    tm = eidx_ref.shape[0]
    lane = lax.broadcasted_iota(I32, (tm, LANES), 1)
    oh0 = (lane == eidx_ref[:, 0:1]).astype(F32)
    oh1 = (lane == eidx_ref[:, 1:2]).astype(F32)
    oh = oh0 + oh1
    row = lax.broadcasted_iota(I32, (tm, tm), 0)
    col = lax.broadcasted_iota(I32, (tm, tm), 1)
    lower = (col < row).astype(BF16)
    rank = jnp.dot(lower, oh.astype(BF16), preferred_element_type=F32)
    posmat = rank + base_ref[...] + run_sc[...]
    for k, ohk in enumerate((oh0, oh1)):
        pk = jnp.sum(ohk * posmat, axis=-1, keepdims=True)
        pk_rows = jnp.broadcast_to(pk, (tm, LANES)).T
        pos_ref[0, k:k + 1, :] = pk_rows[0:1, :].astype(I32)
    run_sc[...] += jnp.sum(oh, axis=0, keepdims=True)


def _positions(eidx, base, *, tm):
    t = eidx.shape[0]
    nt = t // tm
    return pl.pallas_call(
        _pos_kernel,
        grid=(nt,),
        in_specs=[pl.BlockSpec((tm, TOP_K), lambda i: (i, 0)), pl.BlockSpec((1, LANES), lambda i: (0, 0))],
        out_specs=pl.BlockSpec((1, TOP_K, tm), lambda i: (i, 0, 0)),
        out_shape=jax.ShapeDtypeStruct((nt, TOP_K, tm), I32),
        scratch_shapes=[pltpu.VMEM((1, LANES), F32)],
        compiler_params=_cparams(("arbitrary",)),
        name="route_positions",
    )(eidx, base)


def _scatter_kernel(pos_ref, h1_ref, xs_hbm, sem, *, tm):
    def row_copy(t, k):
        return pltpu.make_async_copy(h1_ref.at[pl.ds(t, 1)], xs_hbm.at[pl.ds(pos_ref[0, k, t], 1)], sem)

    def issue(t, c):
        for k in range(TOP_K):
            row_copy(t, k).start()
        return c

    def drain(t, c):
        for k in range(TOP_K):
            row_copy(t, k).wait()
        return c

    lax.fori_loop(0, tm, issue, 0, unroll=8)
    lax.fori_loop(0, tm, drain, 0, unroll=8)


def _scatter_rows(h1, pos, *, tm):
    t, d = h1.shape
    nt = t // tm
    return pl.pallas_call(
        functools.partial(_scatter_kernel, tm=tm),
        grid=(nt,),
        in_specs=[pl.BlockSpec((1, TOP_K, tm), lambda i: (i, 0, 0), memory_space=pltpu.SMEM),
                  pl.BlockSpec((tm, d), lambda i: (i, 0))],
        out_specs=pl.BlockSpec(memory_space=pl.ANY),
        out_shape=jax.ShapeDtypeStruct((TOP_K * t, d), h1.dtype),
        scratch_shapes=[pltpu.SemaphoreType.DMA(())],
        compiler_params=_cparams(("arbitrary",)),
        name="scatter_to_experts",
    )(pos, h1)


def _expert_kernel(tile_ref, grp_ref, lo_ref, hi_ref, first_ref, x_ref, wg_ref, wu_ref, wd_ref, o_ref,
                   wgb, wub, wdb, *, tm):
    w = pl.program_id(0)
    changed = (w == 0) | (grp_ref[w] != grp_ref[jnp.maximum(w - 1, 0)])

    @pl.when(changed)
    def _():
        wgb[...] = wg_ref[0].astype(BF16)
        wub[...] = wu_ref[0].astype(BF16)
        wdb[...] = wd_ref[0].astype(BF16)

    xb = x_ref[...].astype(BF16)
    g = jnp.dot(xb, wgb[...], preferred_element_type=F32)
    u = jnp.dot(xb, wub[...], preferred_element_type=F32)
    hid = (g * _sigmoid(g) * u).astype(BF16)
    o = jnp.dot(hid, wdb[...], preferred_element_type=F32)
    row = tile_ref[w] * tm + lax.broadcasted_iota(I32, (tm, 1), 0)
    o = jnp.where((row >= lo_ref[w]) & (row < hi_ref[w]), o, 0.0)

    @pl.when(first_ref[w] == 1)
    def _():
        o_ref[...] = o

    @pl.when(first_ref[w] == 0)
    def _():
        o_ref[...] += o


def _expert_ffn(xs, w_gate, w_up, w_down, meta, *, tm):
    a, d = xs.shape
    n_experts, _, de = w_gate.shape
    tile_ids, grp_ids, lo, hi, first = meta
    n_work = tile_ids.shape[0]
    grid_spec = pltpu.PrefetchScalarGridSpec(
        num_scalar_prefetch=5,
        grid=(n_work,),
        in_specs=[pl.BlockSpec((tm, d), lambda w, ti, gi, lo_, hi_, fi: (ti[w], 0)),
                  pl.BlockSpec((1, d, de), lambda w, ti, gi, lo_, hi_, fi: (gi[w], 0, 0)),
                  pl.BlockSpec((1, d, de), lambda w, ti, gi, lo_, hi_, fi: (gi[w], 0, 0)),
                  pl.BlockSpec((1, de, d), lambda w, ti, gi, lo_, hi_, fi: (gi[w], 0, 0))],
        out_specs=pl.BlockSpec((tm, d), lambda w, ti, gi, lo_, hi_, fi: (ti[w], 0)),
        scratch_shapes=[pltpu.VMEM((d, de), BF16), pltpu.VMEM((d, de), BF16), pltpu.VMEM((de, d), BF16)],
    )
    return pl.pallas_call(
        functools.partial(_expert_kernel, tm=tm),
        grid_spec=grid_spec,
        out_shape=jax.ShapeDtypeStruct((a, d), F32),
        compiler_params=_cparams(("arbitrary",)),
        name="expert_ffn",
    )(tile_ids, grp_ids, lo, hi, first, xs, w_gate, w_up, w_down)


def _group_metadata(counts, *, n_rows, tm):
    n_experts = counts.shape[0]
    nt = n_rows // tm
    n_work = nt + n_experts - 1
    ends = jnp.cumsum(counts)
    starts = ends - counts
    ntiles_g = jnp.where(counts > 0, (ends - 1) // tm - starts // tm + 1, 0)
    work_end = jnp.cumsum(ntiles_g)
    work_start = work_end - ntiles_g
    total = work_end[-1]
    w = jnp.arange(n_work, dtype=I32)
    wc = jnp.minimum(w, total - 1)
    g = jnp.minimum(jnp.sum((work_end[None, :] <= wc[:, None]).astype(I32), axis=1), n_experts - 1)
    onehot = (g[:, None] == jnp.arange(n_experts, dtype=I32)[None, :]).astype(I32)
    pick = lambda a: jnp.sum(onehot * a[None, :], axis=1)
    starts_g, ends_g = pick(starts), pick(ends)
    tile = (starts_g // tm + (wc - pick(work_start))).astype(I32)
    valid = w < total
    lo = jnp.where(valid, jnp.maximum(starts_g, tile * tm), 0).astype(I32)
    hi = jnp.where(valid, jnp.minimum(ends_g, (tile + 1) * tm), 0).astype(I32)
    prev_tile = jnp.concatenate([jnp.full((1,), -1, I32), tile[:-1]])
    first = (tile != prev_tile).astype(I32)
    return tile, g, lo, hi, first


def _final_kernel(pos_ref, posn_ref, h1_ref, topw_ref, p_ref, os_hbm, l2g_ref, l2b_ref, wpg_ref, bpg_ref,
                  wpp_ref, y_ref, buf, sem, *, tm, alpha):
    i = pl.program_id(0)
    n = pl.num_programs(0)
    slot = i % 2

    def row_copy(pref, sl, t, k):
        return pltpu.make_async_copy(os_hbm.at[pl.ds(pref[0, k, t], 1)], buf.at[sl, k, pl.ds(t, 1)], sem.at[sl])

    def issue(pref, sl):
        def body(t, c):
            for k in range(TOP_K):
                row_copy(pref, sl, t, k).start()
            return c
        lax.fori_loop(0, tm, body, 0, unroll=8)

    @pl.when(i == 0)
    def _():
        issue(pos_ref, 0)

    @pl.when(i + 1 < n)
    def _():
        issue(posn_ref, 1 - slot)

    def drain(t, c):
        for k in range(TOP_K):
            row_copy(pos_ref, slot, t, k).wait()
        return c

    lax.fori_loop(0, tm, drain, 0, unroll=8)
    y = topw_ref[:, 0:1] * buf[slot, 0] + topw_ref[:, 1:2] * buf[slot, 1]
    h2 = _ln(alpha * h1_ref[...] + y, l2g_ref[...], l2b_ref[...])
    gate = _sigmoid(jnp.dot(h2.astype(BF16), wpg_ref[...], preferred_element_type=F32) + bpg_ref[...])
    y_ref[...] = h2 + gate * jnp.dot(p_ref[...].astype(BF16), wpp_ref[...], preferred_element_type=F32)


def _combine_ple(h1, topw, pos, p2d, out_sorted, l2g, l2b, w_pg_b, b_pg, w_pp_b, *, tm, alpha):
    t, d = h1.shape
    nt = t // tm
    ple = p2d.shape[1]
    tok = lambda w: pl.BlockSpec((tm, w), lambda i: (i, 0))
    return pl.pallas_call(
        functools.partial(_final_kernel, tm=tm, alpha=alpha),
        grid=(nt,),
        in_specs=[pl.BlockSpec((1, TOP_K, tm), lambda i: (i, 0, 0), memory_space=pltpu.SMEM),
                  pl.BlockSpec((1, TOP_K, tm), lambda i: (jnp.minimum(i + 1, nt - 1), 0, 0),
                               memory_space=pltpu.SMEM),
                  tok(d), tok(TOP_K), tok(ple), pl.BlockSpec(memory_space=pl.ANY),
                  _resident((1, d)), _resident((1, d)), _resident((d, d)), _resident((1, d)),
                  _resident((ple, d))],
        out_specs=tok(d),
        out_shape=jax.ShapeDtypeStruct((t, d), F32),
        scratch_shapes=[pltpu.VMEM((2, TOP_K, tm, d), F32), pltpu.SemaphoreType.DMA((2,))],
        compiler_params=_cparams(("arbitrary",)),
        name="combine_ple",
    )(pos, pos, h1, topw, p2d, out_sorted, l2g.reshape(1, d), l2b.reshape(1, d), w_pg_b, b_pg.reshape(1, d),
      w_pp_b)


def _post_mix(x2d, att, sgu, p2d, prm, *, tm, alpha, n_experts, n_groups):
    h1, eidx, topw, cnt = _outproj_router(
        x2d, att, sgu, prm["ln_emb_g"], prm["ln_emb_b"], prm["w_o"], prm["ln1_g"], prm["ln1_b"],
        prm["w_r"], prm["b_r"], tm=tm, alpha=alpha, n_experts=n_experts, n_groups=n_groups)
    t = x2d.shape[0]
    counts = cnt[0, :n_experts].astype(I32)
    starts = jnp.cumsum(counts) - counts
    base = jnp.pad(starts.astype(F32), (0, LANES - n_experts)).reshape(1, LANES)
    pos = _positions(eidx, base, tm=tm)
    xs = _scatter_rows(h1, pos, tm=tm)
    meta = _group_metadata(counts, n_rows=TOP_K * t, tm=tm)
    out_sorted = _expert_ffn(xs, prm["w_gate"], prm["w_up"], prm["w_down"], meta, tm=tm)
    return _combine_ple(h1, topw, pos, p2d, out_sorted, prm["ln2_g"], prm["ln2_b"], prm["w_pg"], prm["b_pg"],
                        prm["w_pp"], tm=tm, alpha=alpha)


def kernel(x_prompt, x_sample, cache_k, cache_v, page_table, p_prompt, p_sample, ln_emb_g, ln_emb_b, w_in, lambda_q1, lambda_k1, lambda_q2, lambda_k2, subln_g, rel_bias, sgu_ln_g, sgu_ln_b, sgu_w, sgu_b, w_o, ln1_g, ln1_b, w_router_group, b_router_group, w_router_expert, b_router_expert, w_gate, w_up, w_down, ln2_g, ln2_b, w_ple_gate, b_ple_gate, w_ple_proj):
    bsz, seq, d = x_prompt.shape
    bd, tq_dec, _ = x_sample.shape
    depth, _, page, n_heads, vd = cache_v.shape
    hd = vd // 2
    dw = n_heads * vd
    sw = (w_in.shape[-1] - 3 * dw) // 2
    n_groups = w_router_group.shape[-1]
    n_experts = w_router_expert.shape[-1]
    n_buckets = rel_bias.shape[0]
    ple = p_prompt.shape[-1]
    alpha = (2.0 * depth) ** 0.25
    assert n_experts + n_groups <= LANES

    tp, ts = bsz * seq, bd * tq_dec
    tm_p = min(256, tp)
    tm_s = min(256, ts)
    tq = min(256, seq)
    n_pages = page_table.shape[1]
    pps = math.gcd(n_pages, 8)

    d_const = _const_bucket_distance(n_buckets)
    assert tq + 1 >= d_const and page + 1 >= d_const
    qi = np.arange(tq)[:, None]
    kj = np.arange(tq)[None, :]
    bkt_p = np.concatenate([_t5_bucket(kj - qi, n_buckets), _t5_bucket(tq + kj - qi, n_buckets)], axis=0)
    bias_p = _bias_tables(rel_bias, np.tile(bkt_p, (1, 2)))
    r2 = 2 * tq_dec
    di = np.tile(np.arange(tq_dec), 2)[:, None]
    dj = (np.arange(page * n_heads) // n_heads)[None, :]
    bkt_far = np.full((r2, page * n_heads), n_buckets - 1, np.int32)
    bkt_last = _t5_bucket(page + di - dj, n_buckets)
    bkt_new = np.where((dj <= di) & (dj < tq_dec), _t5_bucket(di - dj, n_buckets), -1).astype(np.int32)
    pen = _bias_tables(rel_bias, np.concatenate([bkt_far, bkt_last, bkt_new], axis=0), head_stride=n_heads)
    pen = pen.reshape(n_heads, 3, r2, page * n_heads).transpose(1, 0, 2, 3).reshape(3, n_heads * r2, page * n_heads)

    xp = x_prompt.reshape(tp, d)
    xs = x_sample.reshape(ts, d)
    hp_x, hs_x = xp, xs
    k_p_rows, v_p_rows, k_s_rows, v_s_rows, sgu_s_rows = [], [], [], [], []
    for l in range(depth):
        assert depth == 1, "the trunk input of deeper layers is the previous layer's output"
        lam_init = 0.8 - 0.6 * math.exp(-0.3 * l)
        lam_vecs = jnp.stack([lambda_q1[l], lambda_k1[l], lambda_q2[l], lambda_k2[l]])
        w_in_b = w_in[l].astype(BF16)
        pad_r = LANES - n_experts - n_groups
        prm = dict(
            ln_emb_g=ln_emb_g, ln_emb_b=ln_emb_b, w_o=w_o[l].astype(BF16), ln1_g=ln1_g[l], ln1_b=ln1_b[l],
            w_r=jnp.pad(jnp.concatenate([w_router_expert[l], w_router_group[l]], axis=1),
                        ((0, 0), (0, pad_r))).astype(BF16),
            b_r=jnp.pad(jnp.concatenate([b_router_expert[l], b_router_group[l]]), (0, pad_r)).reshape(1, LANES),
            w_gate=w_gate[l], w_up=w_up[l], w_down=w_down[l], ln2_g=ln2_g[l], ln2_b=ln2_b[l],
            w_pg=w_ple_gate[l].astype(BF16), b_pg=b_ple_gate[l], w_pp=w_ple_proj[l].astype(BF16))
        common = dict(dw=dw, sw=sw, n_heads=n_heads, q_scale=hd ** -0.5 * LOG2E)
        qp, kp, vp, kbp, vbp, sgu_p = _inproj(
            hp_x, ln_emb_g, ln_emb_b, w_in_b, sgu_ln_g[l], sgu_ln_b[l], sgu_w[l], sgu_b[l],
            seq=seq, tm=tm_p, q_dtype=BF16, emit_vnorm=False, **common)
        qs, ks, vs, _, _, sgu_s, vn_s = _inproj(
            hs_x, ln_emb_g, ln_emb_b, w_in_b, sgu_ln_g[l], sgu_ln_b[l], sgu_w[l], sgu_b[l],
            seq=tq_dec, tm=tm_s, q_dtype=F32, emit_vnorm=True, **common)
        att_p = _attn_prompt(qp.reshape(bsz, seq, dw), kbp.reshape(bsz, seq, dw), vbp.reshape(bsz, seq, dw),
                             bias_p, lam_vecs, subln_g[l], n_heads=n_heads, tq=tq, nh=math.gcd(n_heads, 2),
                             lam_init=lam_init)
        ck = (cache_k if depth == 1 else cache_k[l]).reshape(-1, page * n_heads, vd)
        cv = (cache_v if depth == 1 else cache_v[l]).reshape(-1, page * n_heads, vd)
        att_s = _attn_decode(qs, ks.reshape(-1, vd), vs.reshape(-1, vd), ck, cv, page_table, pen, lam_vecs,
                             subln_g[l], n_heads=n_heads, tq=tq_dec, pages_per_step=pps, lam_init=lam_init)
        post = dict(alpha=alpha, n_experts=n_experts, n_groups=n_groups)
        hp_x = _post_mix(hp_x, att_p.reshape(tp, dw), sgu_p, p_prompt[l].reshape(tp, ple), prm, tm=tm_p, **post)
        hs_x = _post_mix(hs_x, att_s, sgu_s, p_sample[l].reshape(ts, ple), prm, tm=tm_s, **post)
        k_p_rows.append(kp.reshape(bsz, seq, n_heads, vd))
        v_p_rows.append(vp.reshape(bsz, seq, n_heads, vd))
        k_s_rows.append(ks.reshape(bd, tq_dec, n_heads, vd))
        v_s_rows.append(vs.reshape(bd, tq_dec, n_heads, vd))
        sgu_s_rows.append(vn_s.reshape(bd, tq_dec, sw))
    stack = lambda rows: rows[0][None] if len(rows) == 1 else jnp.stack(rows)
    return (hp_x.reshape(bsz, seq, d), hs_x.reshape(bd, tq_dec, d), stack(k_p_rows), stack(v_p_rows),
            stack(k_s_rows), stack(v_s_rows), stack(sgu_s_rows))
```

```python
import functools
import math

import numpy as np
import jax
import jax.numpy as jnp
from jax import lax
from jax.experimental import pallas as pl
from jax.experimental.pallas import tpu as pltpu

F32 = jnp.float32
BF16 = jnp.bfloat16
I32 = jnp.int32

LN_EPS = 1e-5
NEG_INF = -1e30
MAX_DISTANCE = 128
TOP_K = 2
LANES = 128
ONES_ROWS = 16
LOG2E = math.log2(math.e)
VMEM_LIMIT = 56 * 1024 * 1024


def _cparams(sem, vmem=VMEM_LIMIT):
    return pltpu.CompilerParams(dimension_semantics=sem, vmem_limit_bytes=vmem)


def _ln(x, g, b):
    mu = jnp.mean(x, axis=-1, keepdims=True)
    xc = x - mu
    var = jnp.mean(xc * xc, axis=-1, keepdims=True)
    return xc * lax.rsqrt(var + LN_EPS) * g + b


def _gelu_tanh(x):
    c = math.sqrt(2.0 / math.pi)
    return 0.5 * x * (1.0 + jnp.tanh(c * (x + 0.044715 * (x * x * x))))


def _sigmoid(x):
    return 1.0 / (1.0 + jnp.exp(-x))


def _resident(shape):
    nd = len(shape)
    return pl.BlockSpec(shape, lambda *_: (0,) * nd, pipeline_mode=pl.Buffered(1))


def _t5_bucket(dist, n_buckets):
    def run(ft):
        n = np.maximum(dist, 0)
        max_exact = n_buckets // 2
        nf = np.maximum(n, max_exact).astype(ft)
        large = max_exact + (np.log(nf / ft(max_exact)) / ft(math.log(MAX_DISTANCE / max_exact))
                             * ft(n_buckets - max_exact)).astype(np.int32)
        large = np.minimum(large, n_buckets - 1)
        return np.where(n < max_exact, n, large).astype(np.int32)
    b32, b64 = run(np.float32), run(np.float64)
    assert (b32 == b64).all(), "bucket boundaries are precision sensitive"
    return b32


def _const_bucket_distance(n_buckets):
    d = np.arange(0, 4 * MAX_DISTANCE)
    b = _t5_bucket(d, n_buckets)
    below = np.nonzero(b != n_buckets - 1)[0]
    return int(below.max()) + 1


def _bias_kernel(rb_ref, bkt_ref, out_ref, *, n_buckets, head_stride):
    h = pl.program_id(0)
    bkt = bkt_ref[...]
    c_last = rb_ref[n_buckets - 1, h]
    acc = jnp.zeros(bkt.shape, F32)
    for b in range(n_buckets - 1):
        acc = jnp.where(bkt == b, (rb_ref[b, h] - c_last) * LOG2E, acc)
    dead = bkt < 0
    if head_stride:
        col = lax.broadcasted_iota(I32, bkt.shape, 1)
        dead = dead | ((col & (head_stride - 1)) != h)
    out_ref[0] = jnp.where(dead, NEG_INF, acc)


def _bias_tables(rel_bias, bucket_np, head_stride=0):
    n_buckets, n_heads = rel_bias.shape
    r, c = bucket_np.shape
    assert head_stride & (head_stride - 1) == 0
    return pl.pallas_call(
        functools.partial(_bias_kernel, n_buckets=n_buckets, head_stride=head_stride),
        grid=(n_heads,),
        in_specs=[pl.BlockSpec(memory_space=pltpu.SMEM),
                  pl.BlockSpec((r, c), lambda h: (0, 0))],
        out_specs=pl.BlockSpec((1, r, c), lambda h: (h, 0, 0)),
        out_shape=jax.ShapeDtypeStruct((n_heads, r, c), F32),
        compiler_params=_cparams(("arbitrary",)),
        name="bias_tables",
    )(rel_bias, jnp.asarray(bucket_np))


def _inproj_kernel(x_ref, lng_ref, lnb_ref, w_ref, sg_ref, sb_ref, wmix_ref, bmix_ref,
                   q_ref, k4_ref, v4_ref, kb_ref, vb_ref, s_ref, *vn_refs, chunk, dw, sw, n_groups, q_scale):
    xn = _ln(x_ref[...], lng_ref[...], lnb_ref[...])
    xb = xn.astype(BF16)

    def proj(lo, n):
        return jnp.dot(xb, w_ref[:, lo:lo + n], preferred_element_type=F32)

    q_ref[...] = (proj(0, dw) * q_scale).astype(q_ref.dtype)
    n_heads, vd = k4_ref.shape[1:]
    for lo, r4, rb in ((dw, k4_ref, kb_ref), (2 * dw, v4_ref, vb_ref)):
        val = proj(lo, dw)
        rb[...] = val.astype(BF16)
        for h in range(n_heads):
            r4[:, h, :] = val[:, h * vd:(h + 1) * vd]
    u = _gelu_tanh(proj(3 * dw, sw))
    vn = _ln(_gelu_tanh(proj(3 * dw + sw, sw)), sg_ref[...], sb_ref[...])
    if vn_refs:
        vn_refs[0][...] = vn
    vb = vn.astype(BF16)
    tm = xb.shape[0]
    shift = chunk.bit_length() - 1
    row = lax.broadcasted_iota(I32, (tm, tm), 0)
    col = lax.broadcasted_iota(I32, (tm, tm), 1)
    causal = ((row >> shift) == (col >> shift)) & (col <= row)
    gw = sw // n_groups
    for g in range(n_groups):
        mg = jnp.where(causal, wmix_ref[g], 0.0).astype(BF16)
        sv = jnp.dot(mg, vb[:, g * gw:(g + 1) * gw], preferred_element_type=F32) + bmix_ref[:, g:g + 1]
        s_ref[:, g * gw:(g + 1) * gw] = (u[:, g * gw:(g + 1) * gw] * sv).astype(s_ref.dtype)


def _inproj(x2d, ln_g, ln_b, w_in_b, sgu_g, sgu_b_ln, sgu_w, sgu_bias, *, seq, tm, dw, sw, n_heads,
            q_scale, q_dtype, emit_vnorm):
    t, d = x2d.shape
    vd = dw // n_heads
    n_groups, chunk_full, _ = sgu_w.shape
    chunk = min(chunk_full, seq)
    assert chunk & (chunk - 1) == 0 and seq % chunk == 0 and tm % chunk == 0 and t % tm == 0
    rep = tm // chunk
    wmix = jnp.tile(sgu_w[:, :chunk, :chunk], (1, rep, rep))
    bmix = jnp.tile(sgu_bias[:, :chunk].T, (rep, 1))
    cols = w_in_b.shape[1]
    tok = lambda w: pl.BlockSpec((tm, w), lambda i: (i, 0))
    tok4 = pl.BlockSpec((tm, n_heads, vd), lambda i: (i, 0, 0))
    out_shape = [jax.ShapeDtypeStruct((t, dw), q_dtype), jax.ShapeDtypeStruct((t, n_heads, vd), F32),
                 jax.ShapeDtypeStruct((t, n_heads, vd), F32), jax.ShapeDtypeStruct((t, dw), BF16),
                 jax.ShapeDtypeStruct((t, dw), BF16), jax.ShapeDtypeStruct((t, sw), BF16)]
    out_specs = [tok(dw), tok4, tok4, tok(dw), tok(dw), tok(sw)]
    if emit_vnorm:
        out_shape.append(jax.ShapeDtypeStruct((t, sw), F32))
        out_specs.append(tok(sw))
    return pl.pallas_call(
        functools.partial(_inproj_kernel, chunk=chunk, dw=dw, sw=sw, n_groups=n_groups, q_scale=q_scale),
        grid=(t // tm,),
        in_specs=[tok(d), _resident((1, d)), _resident((1, d)), _resident((d, cols)),
                  _resident((1, sw)), _resident((1, sw)), _resident((n_groups, tm, tm)),
                  _resident((tm, n_groups))],
        out_specs=out_specs,
        out_shape=out_shape,
        compiler_params=_cparams(("arbitrary",)),
        name="inproj_sgu",
    )(x2d, ln_g.reshape(1, d), ln_b.reshape(1, d), w_in_b, sgu_g.reshape(1, sw), sgu_b_ln.reshape(1, sw),
      wmix, bmix)


def _diff_lambda_in_kernel(lam_ref, lam_init):
    lv = lam_ref[...]
    a = jnp.sum(lv[0:1] * lv[1:2], axis=-1, keepdims=True)
    b = jnp.sum(lv[2:3] * lv[3:4], axis=-1, keepdims=True)
    return jnp.exp(a) - jnp.exp(b) + lam_init


def _attn_prompt_kernel(q_ref, k_ref, v_ref, bias_ref, lam_ref, g_ref, o_ref, vt_sc, sa_sc, sb_sc, m_sc, acc_sc,
                        *, tq, hd, nh, lam_init):
    qi = pl.program_id(2)
    nk = vt_sc.shape[1]
    vd = 2 * hd
    heads = range(nh)
    hcols = lambda hh: slice(hh * vd, (hh + 1) * vd)

    @pl.when(qi == 0)
    def _():
        for hh in heads:
            for c in range(nk):
                vt_sc[hh, c, 0:vd, :] = v_ref[0, c * tq:(c + 1) * tq, hcols(hh)].astype(F32).T.astype(BF16)
                vt_sc[hh, c, vd:, :] = jnp.ones((ONES_ROWS, tq), BF16)

    sub = lax.broadcasted_iota(I32, (vd, tq), 0)
    qq = []
    for hh in heads:
        qt = q_ref[0, :, hcols(hh)].astype(F32).T
        qq.append(jnp.concatenate([jnp.where(sub < hd, qt, 0.0), jnp.where(sub >= hd, qt, 0.0)],
                                  axis=1).astype(BF16))

    def scores_to(s_sc, j):
        rows = pl.ds(pl.multiple_of(j * tq, tq), tq)
        for hh in heads:
            s_sc[hh] = jnp.dot(k_ref[0, rows, hcols(hh)], qq[hh], preferred_element_type=F32)

    def update_from(s_sc, j):
        d = qi - j
        kind = jnp.where(d < 0, 3, jnp.minimum(d, 2))
        j = jnp.minimum(j, nk - 1)
        for hh in heads:
            s = s_sc[hh] + bias_ref[hh, kind]
            m = m_sc[hh]
            m_new = jnp.maximum(m, jnp.max(s, axis=0, keepdims=True))
            p = jnp.exp2((s - m_new).astype(BF16))
            acc_sc[hh] = acc_sc[hh] * jnp.exp2(m - m_new) + jnp.dot(vt_sc[hh, j], p, preferred_element_type=F32)
            m_sc[hh] = m_new

    m_sc[...] = jnp.full(m_sc.shape, NEG_INF, F32)
    acc_sc[...] = jnp.zeros(acc_sc.shape, F32)
    scores_to(sa_sc, 0)

    def pair(i, c):
        j = 2 * i
        scores_to(sb_sc, jnp.minimum(j + 1, nk - 1))
        update_from(sa_sc, j)
        scores_to(sa_sc, jnp.minimum(j + 2, nk - 1))
        update_from(sb_sc, j + 1)
        return c

    lax.fori_loop(0, (qi + 2) // 2, pair, 0)
    lam = _diff_lambda_in_kernel(lam_ref, lam_init)
    for hh in heads:
        acc = acc_sc[hh]
        l = acc[vd:vd + 1, :]
        ot = acc[0:vd, :tq] / l[:, :tq] - lam * (acc[0:vd, tq:] / l[:, tq:])
        ms = jnp.mean(ot * ot, axis=0, keepdims=True)
        ot = ot * lax.rsqrt(ms + LN_EPS) * g_ref[...] * (1.0 - lam_init)
        o_ref[0, :, hcols(hh)] = ot.T.astype(o_ref.dtype)


def _attn_prompt(q, k, v, bias_p, lam_vecs, subln_g, *, n_heads, tq, nh, lam_init):
    b, s, dw = q.shape
    vd = dw // n_heads
    hd = vd // 2
    nq = s // tq
    assert n_heads % nh == 0
    return pl.pallas_call(
        functools.partial(_attn_prompt_kernel, tq=tq, hd=hd, nh=nh, lam_init=lam_init),
        grid=(b, n_heads // nh, nq),
        in_specs=[pl.BlockSpec((1, tq, nh * vd), lambda bi, h, qi: (bi, qi, h)),
                  pl.BlockSpec((1, s, nh * vd), lambda bi, h, qi: (bi, 0, h)),
                  pl.BlockSpec((1, s, nh * vd), lambda bi, h, qi: (bi, 0, h)),
                  pl.BlockSpec((nh, 4, tq, 2 * tq), lambda bi, h, qi: (h, 0, 0, 0)),
                  pl.BlockSpec((4, hd), lambda bi, h, qi: (0, 0)),
                  pl.BlockSpec((vd, 1), lambda bi, h, qi: (0, 0))],
        out_specs=pl.BlockSpec((1, tq, nh * vd), lambda bi, h, qi: (bi, qi, h)),
        out_shape=jax.ShapeDtypeStruct((b, s, dw), BF16),
        scratch_shapes=[pltpu.VMEM((nh, nq, vd + ONES_ROWS, tq), BF16), pltpu.VMEM((nh, tq, 2 * tq), F32),
                        pltpu.VMEM((nh, tq, 2 * tq), F32), pltpu.VMEM((nh, 1, 2 * tq), F32),
                        pltpu.VMEM((nh, vd + ONES_ROWS, 2 * tq), F32)],
        compiler_params=_cparams(("arbitrary", "arbitrary", "arbitrary")),
        name="attn_prompt",
    )(q, k, v, bias_p, lam_vecs, subln_g.reshape(vd, 1))


def _attn_decode_kernel(pt_ref, *refs, n_page_refs, n_heads, tq, hd, lam_init):
    k_refs = refs[:n_page_refs]
    v_refs = refs[n_page_refs:2 * n_page_refs]
    (q_ref, kn_ref, vn_ref, pen_ref, lam_ref, g_ref, o_ref, qall_sc, m_sc, l_sc, acc_sc) = refs[2 * n_page_refs:]
    del pt_ref
    step = pl.program_id(1)
    last = pl.num_programs(1) - 1
    vd = 2 * hd
    r2 = 2 * tq
    nt_dims = (((1,), (1,)), ((), ()))

    @pl.when(step == 0)
    def _():
        lane = lax.broadcasted_iota(I32, (tq, vd), 1)
        rows = []
        for h in range(n_heads):
            qh = q_ref[:, h * vd:(h + 1) * vd]
            rows += [jnp.where(lane < hd, qh, 0.0), jnp.where(lane >= hd, qh, 0.0)]
        qall_sc[...] = jnp.concatenate(rows, axis=0).astype(BF16)
        m_sc[...] = jnp.full(m_sc.shape, NEG_INF, F32)
        l_sc[...] = jnp.zeros(l_sc.shape, F32)
        acc_sc[...] = jnp.zeros(acc_sc.shape, F32)

    def update(carry, blocks):
        m, l, acc = carry
        ss = [lax.dot_general(qall_sc[...], k2.astype(BF16), nt_dims, preferred_element_type=F32) + pen
              for k2, _, pen in blocks]
        smax = functools.reduce(jnp.maximum, ss)
        m_new = jnp.maximum(m, jnp.max(smax, axis=-1, keepdims=True))
        corr = jnp.exp2(m - m_new)
        acc = acc * corr
        psum = None
        for s, (_, v2, _) in zip(ss, blocks):
            p = jnp.exp2(s - m_new)
            psum = p if psum is None else psum + p
            acc = acc + jnp.dot(p.astype(BF16), v2.astype(BF16), preferred_element_type=F32)
        l = l * corr + jnp.sum(psum, axis=-1, keepdims=True)
        return m_new, l, acc

    pens = [pen_ref[0]] * (n_page_refs - 1) + [pen_ref[jnp.where(step == last, 1, 0)]]
    carry = update((m_sc[...], l_sc[...], acc_sc[...]),
                   [(k_refs[i][0], v_refs[i][0], pens[i]) for i in range(n_page_refs)])
    m_sc[...], l_sc[...], acc_sc[...] = carry

    @pl.when(step == last)
    def _():
        n_new = kn_ref.shape[0]
        pad = jnp.zeros((LANES - n_new, vd), F32)
        k2 = jnp.concatenate([kn_ref[...], pad], axis=0)
        v2 = jnp.concatenate([vn_ref[...], pad], axis=0)
        _, l, acc = update(carry, [(k2, v2, pen_ref[2, :, 0:LANES])])
        lam = _diff_lambda_in_kernel(lam_ref, lam_init)
        for h in range(n_heads):
            r0 = h * r2
            o = acc[r0:r0 + tq] / l[r0:r0 + tq] - lam * (acc[r0 + tq:r0 + r2] / l[r0 + tq:r0 + r2])
            ms = jnp.mean(o * o, axis=-1, keepdims=True)
            o_ref[:, h * vd:(h + 1) * vd] = (o * lax.rsqrt(ms + LN_EPS) * g_ref[...] * (1.0 - lam_init)
                                             ).astype(o_ref.dtype)


def _attn_decode(q, k_new, v_new, cache_k2, cache_v2, page_table, pen, lam_vecs, subln_g,
                 *, n_heads, tq, pages_per_step, lam_init):
    t, dw = q.shape
    bd, n_pages = page_table.shape
    _, page_rows, vd = cache_k2.shape
    hd = vd // 2
    pps = pages_per_step
    n_new = tq * n_heads
    rows = n_heads * 2 * tq
    assert n_pages % pps == 0 and tq % 8 == 0 and n_new <= LANES <= page_rows
    n_steps = n_pages // pps

    def page_spec(i):
        return pl.BlockSpec((1, page_rows, vd), lambda b, s, pt: (pt[b * n_pages + s * pps + i], 0, 0))

    tokb = pl.BlockSpec((tq, dw), lambda b, s, pt: (b, 0))
    newb = pl.BlockSpec((n_new, vd), lambda b, s, pt: (b, 0))
    full = lambda shape: pl.BlockSpec(shape, lambda b, s, pt: (0,) * len(shape))
    grid_spec = pltpu.PrefetchScalarGridSpec(
        num_scalar_prefetch=1,
        grid=(bd, n_steps),
        in_specs=[page_spec(i) for i in range(pps)] + [page_spec(i) for i in range(pps)]
        + [tokb, newb, newb, full(pen.shape), full((4, hd)), full((1, vd))],
        out_specs=tokb,
        scratch_shapes=[pltpu.VMEM((rows, vd), BF16), pltpu.VMEM((rows, 1), F32),
                        pltpu.VMEM((rows, 1), F32), pltpu.VMEM((rows, vd), F32)],
    )
    return pl.pallas_call(
        functools.partial(_attn_decode_kernel, n_page_refs=pps, n_heads=n_heads, tq=tq, hd=hd, lam_init=lam_init),
        grid_spec=grid_spec,
        out_shape=jax.ShapeDtypeStruct((t, dw), F32),
        compiler_params=_cparams(("arbitrary", "arbitrary")),
        name="attn_decode",
    )(page_table.reshape(-1), *([cache_k2] * pps), *([cache_v2] * pps), q, k_new, v_new, pen, lam_vecs,
      subln_g.reshape(1, vd))


def _outproj_kernel(x_ref, att_ref, sgu_ref, lng_ref, lnb_ref, wo_ref, l1g_ref, l1b_ref, wr_ref, br_ref,
                    h1_ref, eidx_ref, topw_ref, cnt_ref, *, alpha, dw, n_experts, n_groups):
    i = pl.program_id(0)
    hp = _ln(x_ref[...], lng_ref[...], lnb_ref[...])
    mix = (jnp.dot(att_ref[...].astype(BF16), wo_ref[0:dw, :], preferred_element_type=F32)
           + jnp.dot(sgu_ref[...].astype(BF16), wo_ref[dw:, :], preferred_element_type=F32))
    h1 = _ln(alpha * hp + mix, l1g_ref[...], l1b_ref[...])
    h1_ref[...] = h1
    logits = jnp.dot(h1.astype(BF16), wr_ref[...], preferred_element_type=F32) + br_ref[...]
    tm = logits.shape[0]
    epg = n_experts // n_groups
    lane = lax.broadcasted_iota(I32, (tm, LANES), 1).astype(F32)
    big = float(LANES)
    ninf = -jnp.inf
    is_g = (lane >= n_experts) & (lane < n_experts + n_groups)
    gmax = jnp.max(jnp.where(is_g, logits, ninf), axis=-1, keepdims=True)
    gidx = jnp.min(jnp.where(is_g & (logits == gmax), lane, big), axis=-1, keepdims=True) - n_experts
    gsum = jnp.sum(jnp.where(is_g, jnp.exp(logits - gmax), 0.0), axis=-1, keepdims=True)
    g_w = 1.0 / gsum
    sel = (lane >= gidx * epg) & (lane < (gidx + 1.0) * epg)
    v1 = jnp.max(jnp.where(sel, logits, ninf), axis=-1, keepdims=True)
    i1 = jnp.min(jnp.where(sel & (logits == v1), lane, big), axis=-1, keepdims=True)
    sel2 = sel & (lane != i1)
    v2 = jnp.max(jnp.where(sel2, logits, ninf), axis=-1, keepdims=True)
    i2 = jnp.min(jnp.where(sel2 & (logits == v2), lane, big), axis=-1, keepdims=True)
    e = jnp.exp(v2 - v1)
    eidx_ref[:, 0:1] = i1.astype(I32)
    eidx_ref[:, 1:2] = i2.astype(I32)
    topw_ref[:, 0:1] = g_w / (1.0 + e)
    topw_ref[:, 1:2] = g_w * e / (1.0 + e)
    onehot = ((lane == i1) | (lane == i2)).astype(F32)
    cnt = jnp.sum(onehot, axis=0, keepdims=True)

    @pl.when(i == 0)
    def _():
        cnt_ref[...] = jnp.zeros(cnt_ref.shape, F32)

    cnt_ref[...] += jnp.broadcast_to(cnt, cnt_ref.shape)


def _outproj_router(x2d, att, sgu, ln_g, ln_b, w_o_b, l1g, l1b, w_r_b, b_r, *, tm, alpha, n_experts, n_groups):
    t, d = x2d.shape
    dw = att.shape[1]
    sw = sgu.shape[1]
    tok = lambda w: pl.BlockSpec((tm, w), lambda i: (i, 0))
    return pl.pallas_call(
        functools.partial(_outproj_kernel, alpha=alpha, dw=dw, n_experts=n_experts, n_groups=n_groups),
        grid=(t // tm,),
        in_specs=[tok(d), tok(dw), tok(sw), _resident((1, d)), _resident((1, d)), _resident((dw + sw, d)),
                  _resident((1, d)), _resident((1, d)), _resident((d, LANES)), _resident((1, LANES))],
        out_specs=[tok(d), tok(TOP_K), tok(TOP_K), pl.BlockSpec((8, LANES), lambda i: (0, 0))],
        out_shape=[jax.ShapeDtypeStruct((t, d), F32), jax.ShapeDtypeStruct((t, TOP_K), I32),
                   jax.ShapeDtypeStruct((t, TOP_K), F32), jax.ShapeDtypeStruct((8, LANES), F32)],
        compiler_params=_cparams(("arbitrary",)),
        name="outproj_router",
    )(x2d, att, sgu, ln_g.reshape(1, d), ln_b.reshape(1, d), w_o_b, l1g.reshape(1, d), l1b.reshape(1, d),
      w_r_b, b_r)


def _pos_kernel(eidx_ref, base_ref, pos_ref, run_sc):
    i = pl.program_id(0)

    @pl.when(i == 0)
    def _():
        run_sc[...] = jnp.zeros(run_sc.shape, F32)

    tm = eidx_ref.shape[0]
    lane = lax.broadcasted_iota(I32, (tm, LANES), 1)
    oh0 = (lane == eidx_ref[:, 0:1]).astype(F32)
    oh1 = (lane == eidx_ref[:, 1:2]).astype(F32)
    oh = oh0 + oh1
    row = lax.broadcasted_iota(I32, (tm, tm), 0)
    col = lax.broadcasted_iota(I32, (tm, tm), 1)
    lower = (col < row).astype(BF16)
    rank = jnp.dot(lower, oh.astype(BF16), preferred_element_type=F32)
    posmat = rank + base_ref[...] + run_sc[...]
    for k, ohk in enumerate((oh0, oh1)):
        pk = jnp.sum(ohk * posmat, axis=-1, keepdims=True)
        pk_rows = jnp.broadcast_to(pk, (tm, LANES)).T
        pos_ref[0, k:k + 1, :] = pk_rows[0:1, :].astype(I32)
    run_sc[...] += jnp.sum(oh, axis=0, keepdims=True)


def _positions(eidx, base, *, tm):
    t = eidx.shape[0]
    nt = t // tm
    return pl.pallas_call(
        _pos_kernel,
        grid=(nt,),
        in_specs=[pl.BlockSpec((tm, TOP_K), lambda i: (i, 0)), pl.BlockSpec((1, LANES), lambda i: (0, 0))],
        out_specs=pl.BlockSpec((1, TOP_K, tm), lambda i: (i, 0, 0)),
        out_shape=jax.ShapeDtypeStruct((nt, TOP_K, tm), I32),
        scratch_shapes=[pltpu.VMEM((1, LANES), F32)],
        compiler_params=_cparams(("arbitrary",)),
        name="route_positions",
    )(eidx, base)


def _scatter_kernel(pos_ref, h1_ref, xs_hbm, sem, *, tm):
    def row_copy(t, k):
        return pltpu.make_async_copy(h1_ref.at[pl.ds(t, 1)], xs_hbm.at[pl.ds(pos_ref[0, k, t], 1)], sem)

    def issue(t, c):
        for k in range(TOP_K):
            row_copy(t, k).start()
        return c

    def drain(t, c):
        for k in range(TOP_K):
            row_copy(t, k).wait()
        return c

    lax.fori_loop(0, tm, issue, 0, unroll=8)
    lax.fori_loop(0, tm, drain, 0, unroll=8)


def _scatter_rows(h1, pos, *, tm):
    t, d = h1.shape
    nt = t // tm
    return pl.pallas_call(
        functools.partial(_scatter_kernel, tm=tm),
        grid=(nt,),
        in_specs=[pl.BlockSpec((1, TOP_K, tm), lambda i: (i, 0, 0), memory_space=pltpu.SMEM),
                  pl.BlockSpec((tm, d), lambda i: (i, 0))],
        out_specs=pl.BlockSpec(memory_space=pl.ANY),
        out_shape=jax.ShapeDtypeStruct((TOP_K * t, d), h1.dtype),
        scratch_shapes=[pltpu.SemaphoreType.DMA(())],
        compiler_params=_cparams(("arbitrary",)),
        name="scatter_to_experts",
    )(pos, h1)


def _expert_kernel(tile_ref, grp_ref, lo_ref, hi_ref, first_ref, x_ref, wg_ref, wu_ref, wd_ref, o_ref,
                   wgb, wub, wdb, *, tm):
    w = pl.program_id(0)
    changed = (w == 0) | (grp_ref[w] != grp_ref[jnp.maximum(w - 1, 0)])

    @pl.when(changed)
    def _():
        wgb[...] = wg_ref[0].astype(BF16)
        wub[...] = wu_ref[0].astype(BF16)
        wdb[...] = wd_ref[0].astype(BF16)

    xb = x_ref[...].astype(BF16)
    g = jnp.dot(xb, wgb[...], preferred_element_type=F32)
    u = jnp.dot(xb, wub[...], preferred_element_type=F32)
    hid = (g * _sigmoid(g) * u).astype(BF16)
    o = jnp.dot(hid, wdb[...], preferred_element_type=F32)
    row = tile_ref[w] * tm + lax.broadcasted_iota(I32, (tm, 1), 0)
    o = jnp.where((row >= lo_ref[w]) & (row < hi_ref[w]), o, 0.0)

    @pl.when(first_ref[w] == 1)
    def _():
        o_ref[...] = o

    @pl.when(first_ref[w] == 0)
    def _():
        o_ref[...] += o


def _expert_ffn(xs, w_gate, w_up, w_down, meta, *, tm):
    a, d = xs.shape
    n_experts, _, de = w_gate.shape
    tile_ids, grp_ids, lo, hi, first = meta
    n_work = tile_ids.shape[0]
    grid_spec = pltpu.PrefetchScalarGridSpec(
        num_scalar_prefetch=5,
        grid=(n_work,),
        in_specs=[pl.BlockSpec((tm, d), lambda w, ti, gi, lo_, hi_, fi: (ti[w], 0)),
                  pl.BlockSpec((1, d, de), lambda w, ti, gi, lo_, hi_, fi: (gi[w], 0, 0)),
                  pl.BlockSpec((1, d, de), lambda w, ti, gi, lo_, hi_, fi: (gi[w], 0, 0)),
                  pl.BlockSpec((1, de, d), lambda w, ti, gi, lo_, hi_, fi: (gi[w], 0, 0))],
        out_specs=pl.BlockSpec((tm, d), lambda w, ti, gi, lo_, hi_, fi: (ti[w], 0)),
        scratch_shapes=[pltpu.VMEM((d, de), BF16), pltpu.VMEM((d, de), BF16), pltpu.VMEM((de, d), BF16)],
    )
    return pl.pallas_call(
        functools.partial(_expert_kernel, tm=tm),
        grid_spec=grid_spec,
        out_shape=jax.ShapeDtypeStruct((a, d), F32),
        compiler_params=_cparams(("arbitrary",)),
        name="expert_ffn",
    )(tile_ids, grp_ids, lo, hi, first, xs, w_gate, w_up, w_down)


def _group_metadata(counts, *, n_rows, tm):
    n_experts = counts.shape[0]
    nt = n_rows // tm
    n_work = nt + n_experts - 1
    ends = jnp.cumsum(counts)
    starts = ends - counts
    ntiles_g = jnp.where(counts > 0, (ends - 1) // tm - starts // tm + 1, 0)
    work_end = jnp.cumsum(ntiles_g)
    work_start = work_end - ntiles_g
    total = work_end[-1]
    w = jnp.arange(n_work, dtype=I32)
    wc = jnp.minimum(w, total - 1)
    g = jnp.minimum(jnp.sum((work_end[None, :] <= wc[:, None]).astype(I32), axis=1), n_experts - 1)
    onehot = (g[:, None] == jnp.arange(n_experts, dtype=I32)[None, :]).astype(I32)
    pick = lambda a: jnp.sum(onehot * a[None, :], axis=1)
    starts_g, ends_g = pick(starts), pick(ends)
    tile = (starts_g // tm + (wc - pick(work_start))).astype(I32)
    valid = w < total
    lo = jnp.where(valid, jnp.maximum(starts_g, tile * tm), 0).astype(I32)
    hi = jnp.where(valid, jnp.minimum(ends_g, (tile + 1) * tm), 0).astype(I32)
    prev_tile = jnp.concatenate([jnp.full((1,), -1, I32), tile[:-1]])
    first = (tile != prev_tile).astype(I32)
    return tile, g, lo, hi, first


def _final_kernel(pos_ref, posn_ref, h1_ref, topw_ref, p_ref, os_hbm, l2g_ref, l2b_ref, wpg_ref, bpg_ref,
                  wpp_ref, y_ref, buf, sem, *, tm, alpha):
    i = pl.program_id(0)
    n = pl.num_programs(0)
    slot = i % 2

    def row_copy(pref, sl, t, k):
        return pltpu.make_async_copy(os_hbm.at[pl.ds(pref[0, k, t], 1)], buf.at[sl, k, pl.ds(t, 1)], sem.at[sl])

    def issue(pref, sl):
        def body(t, c):
            for k in range(TOP_K):
                row_copy(pref, sl, t, k).start()
            return c
        lax.fori_loop(0, tm, body, 0, unroll=8)

    @pl.when(i == 0)
    def _():
        issue(pos_ref, 0)

    @pl.when(i + 1 < n)
    def _():
        issue(posn_ref, 1 - slot)

    def drain(t, c):
        for k in range(TOP_K):
            row_copy(pos_ref, slot, t, k).wait()
        return c

    lax.fori_loop(0, tm, drain, 0, unroll=8)
    y = topw_ref[:, 0:1] * buf[slot, 0] + topw_ref[:, 1:2] * buf[slot, 1]
    h2 = _ln(alpha * h1_ref[...] + y, l2g_ref[...], l2b_ref[...])
    gate = _sigmoid(jnp.dot(h2.astype(BF16), wpg_ref[...], preferred_element_type=F32) + bpg_ref[...])
    y_ref[...] = h2 + gate * jnp.dot(p_ref[...].astype(BF16), wpp_ref[...], preferred_element_type=F32)


def _combine_ple(h1, topw, pos, p2d, out_sorted, l2g, l2b, w_pg_b, b_pg, w_pp_b, *, tm, alpha):
    t, d = h1.shape
    nt = t // tm
    ple = p2d.shape[1]
    tok = lambda w: pl.BlockSpec((tm, w), lambda i: (i, 0))
    return pl.pallas_call(
        functools.partial(_final_kernel, tm=tm, alpha=alpha),
        grid=(nt,),
        in_specs=[pl.BlockSpec((1, TOP_K, tm), lambda i: (i, 0, 0), memory_space=pltpu.SMEM),
                  pl.BlockSpec((1, TOP_K, tm), lambda i: (jnp.minimum(i + 1, nt - 1), 0, 0),
                               memory_space=pltpu.SMEM),
                  tok(d), tok(TOP_K), tok(ple), pl.BlockSpec(memory_space=pl.ANY),
                  _resident((1, d)), _resident((1, d)), _resident((d, d)), _resident((1, d)),
                  _resident((ple, d))],
        out_specs=tok(d),
        out_shape=jax.ShapeDtypeStruct((t, d), F32),
        scratch_shapes=[pltpu.VMEM((2, TOP_K, tm, d), F32), pltpu.SemaphoreType.DMA((2,))],
        compiler_params=_cparams(("arbitrary",)),
        name="combine_ple",
    )(pos, pos, h1, topw, p2d, out_sorted, l2g.reshape(1, d), l2b.reshape(1, d), w_pg_b, b_pg.reshape(1, d),
      w_pp_b)


def _post_mix(x2d, att, sgu, p2d, prm, *, tm, alpha, n_experts, n_groups):
    h1, eidx, topw, cnt = _outproj_router(
        x2d, att, sgu, prm["ln_emb_g"], prm["ln_emb_b"], prm["w_o"], prm["ln1_g"], prm["ln1_b"],
        prm["w_r"], prm["b_r"], tm=tm, alpha=alpha, n_experts=n_experts, n_groups=n_groups)
    t = x2d.shape[0]
    counts = cnt[0, :n_experts].astype(I32)
    starts = jnp.cumsum(counts) - counts
    base = jnp.pad(starts.astype(F32), (0, LANES - n_experts)).reshape(1, LANES)
    pos = _positions(eidx, base, tm=tm)
    xs = _scatter_rows(h1, pos, tm=tm)
    meta = _group_metadata(counts, n_rows=TOP_K * t, tm=tm)
    out_sorted = _expert_ffn(xs, prm["w_gate"], prm["w_up"], prm["w_down"], meta, tm=tm)
    return _combine_ple(h1, topw, pos, p2d, out_sorted, prm["ln2_g"], prm["ln2_b"], prm["w_pg"], prm["b_pg"],
                        prm["w_pp"], tm=tm, alpha=alpha)


def kernel(x_prompt, x_sample, cache_k, cache_v, page_table, p_prompt, p_sample, ln_emb_g, ln_emb_b, w_in, lambda_q1, lambda_k1, lambda_q2, lambda_k2, subln_g, rel_bias, sgu_ln_g, sgu_ln_b, sgu_w, sgu_b, w_o, ln1_g, ln1_b, w_router_group, b_router_group, w_router_expert, b_router_expert, w_gate, w_up, w_down, ln2_g, ln2_b, w_ple_gate, b_ple_gate, w_ple_proj):
    bsz, seq, d = x_prompt.shape
    bd, tq_dec, _ = x_sample.shape
    depth, _, page, n_heads, vd = cache_v.shape
    hd = vd // 2
    dw = n_heads * vd
    sw = (w_in.shape[-1] - 3 * dw) // 2
    n_groups = w_router_group.shape[-1]
    n_experts = w_router_expert.shape[-1]
    n_buckets = rel_bias.shape[0]
    ple = p_prompt.shape[-1]
    alpha = (2.0 * depth) ** 0.25
    assert n_experts + n_groups <= LANES

    tp, ts = bsz * seq, bd * tq_dec
    tm_p = min(256, tp)
    tm_s = min(256, ts)
    tq = min(256, seq)
    n_pages = page_table.shape[1]
    pps = math.gcd(n_pages, 8)

    d_const = _const_bucket_distance(n_buckets)
    assert tq + 1 >= d_const and page + 1 >= d_const
    qi = np.arange(tq)[:, None]
    kj = np.arange(tq)[None, :]
    bkt_p = np.concatenate([np.where(qi <= kj, _t5_bucket(kj - qi, n_buckets), -1).astype(np.int32),
                            _t5_bucket(tq + kj - qi, n_buckets), np.full((tq, tq), n_buckets - 1, np.int32),
                            np.full((tq, tq), -1, np.int32)], axis=0)
    bias_p = _bias_tables(rel_bias, np.tile(bkt_p, (1, 2))).reshape(n_heads, 4, tq, 2 * tq)
    r2 = 2 * tq_dec
    di = np.tile(np.arange(tq_dec), 2)[:, None]
    dj = (np.arange(page * n_heads) // n_heads)[None, :]
    bkt_far = np.full((r2, page * n_heads), n_buckets - 1, np.int32)
    bkt_last = _t5_bucket(page + di - dj, n_buckets)
    bkt_new = np.where((dj <= di) & (dj < tq_dec), _t5_bucket(di - dj, n_buckets), -1).astype(np.int32)
    pen = _bias_tables(rel_bias, np.concatenate([bkt_far, bkt_last, bkt_new], axis=0), head_stride=n_heads)
    pen = pen.reshape(n_heads, 3, r2, page * n_heads).transpose(1, 0, 2, 3).reshape(3, n_heads * r2, page * n_heads)

    xp = x_prompt.reshape(tp, d)
    xs = x_sample.reshape(ts, d)
    hp_x, hs_x = xp, xs
    k_p_rows, v_p_rows, k_s_rows, v_s_rows, sgu_s_rows = [], [], [], [], []
    for l in range(depth):
        assert depth == 1, "the trunk input of deeper layers is the previous layer's output"
        lam_init = 0.8 - 0.6 * math.exp(-0.3 * l)
        lam_vecs = jnp.stack([lambda_q1[l], lambda_k1[l], lambda_q2[l], lambda_k2[l]])
        w_in_b = w_in[l].astype(BF16)
        pad_r = LANES - n_experts - n_groups
        prm = dict(
            ln_emb_g=ln_emb_g, ln_emb_b=ln_emb_b, w_o=w_o[l].astype(BF16), ln1_g=ln1_g[l], ln1_b=ln1_b[l],
            w_r=jnp.pad(jnp.concatenate([w_router_expert[l], w_router_group[l]], axis=1),
                        ((0, 0), (0, pad_r))).astype(BF16),
            b_r=jnp.pad(jnp.concatenate([b_router_expert[l], b_router_group[l]]), (0, pad_r)).reshape(1, LANES),
            w_gate=w_gate[l], w_up=w_up[l], w_down=w_down[l], ln2_g=ln2_g[l], ln2_b=ln2_b[l],
            w_pg=w_ple_gate[l].astype(BF16), b_pg=b_ple_gate[l], w_pp=w_ple_proj[l].astype(BF16))
        common = dict(dw=dw, sw=sw, n_heads=n_heads, q_scale=hd ** -0.5 * LOG2E)
        qp, kp, vp, kbp, vbp, sgu_p = _inproj(
            hp_x, ln_emb_g, ln_emb_b, w_in_b, sgu_ln_g[l], sgu_ln_b[l], sgu_w[l], sgu_b[l],
            seq=seq, tm=tm_p, q_dtype=BF16, emit_vnorm=False, **common)
        qs, ks, vs, _, _, sgu_s, vn_s = _inproj(
            hs_x, ln_emb_g, ln_emb_b, w_in_b, sgu_ln_g[l], sgu_ln_b[l], sgu_w[l], sgu_b[l],
            seq=tq_dec, tm=tm_s, q_dtype=F32, emit_vnorm=True, **common)
        att_p = _attn_prompt(qp.reshape(bsz, seq, dw), kbp.reshape(bsz, seq, dw), vbp.reshape(bsz, seq, dw),
                             bias_p, lam_vecs, subln_g[l], n_heads=n_heads, tq=tq, nh=math.gcd(n_heads, 4),
                             lam_init=lam_init)
        ck = (cache_k if depth == 1 else cache_k[l]).reshape(-1, page * n_heads, vd)
        cv = (cache_v if depth == 1 else cache_v[l]).reshape(-1, page * n_heads, vd)
        att_s = _attn_decode(qs, ks.reshape(-1, vd), vs.reshape(-1, vd), ck, cv, page_table, pen, lam_vecs,
                             subln_g[l], n_heads=n_heads, tq=tq_dec, pages_per_step=pps, lam_init=lam_init)
        post = dict(alpha=alpha, n_experts=n_experts, n_groups=n_groups)
        hp_x = _post_mix(hp_x, att_p.reshape(tp, dw), sgu_p, p_prompt[l].reshape(tp, ple), prm, tm=tm_p, **post)
        hs_x = _post_mix(hs_x, att_s, sgu_s, p_sample[l].reshape(ts, ple), prm, tm=tm_s, **post)
        k_p_rows.append(kp.reshape(bsz, seq, n_heads, vd))
        v_p_rows.append(vp.reshape(bsz, seq, n_heads, vd))
        k_s_rows.append(ks.reshape(bd, tq_dec, n_heads, vd))
        v_s_rows.append(vs.reshape(bd, tq_dec, n_heads, vd))
        sgu_s_rows.append(vn_s.reshape(bd, tq_dec, sw))
    stack = lambda rows: rows[0][None] if len(rows) == 1 else jnp.stack(rows)
    return (hp_x.reshape(bsz, seq, d), hs_x.reshape(bd, tq_dec, d), stack(k_p_rows), stack(v_p_rows),
            stack(k_s_rows), stack(v_s_rows), stack(sgu_s_rows))
```

```python
import functools
import math

import numpy as np
import jax
import jax.numpy as jnp
from jax import lax
from jax.experimental import pallas as pl
from jax.experimental.pallas import tpu as pltpu

F32 = jnp.float32
BF16 = jnp.bfloat16
I32 = jnp.int32

LN_EPS = 1e-5
NEG_INF = -1e30
MAX_DISTANCE = 128
TOP_K = 2
LANES = 128
ONES_ROWS = 16
LOG2E = math.log2(math.e)
VMEM_LIMIT = 56 * 1024 * 1024


def _cparams(sem, vmem=VMEM_LIMIT):
    return pltpu.CompilerParams(dimension_semantics=sem, vmem_limit_bytes=vmem)


def _ln(x, g, b):
    mu = jnp.mean(x, axis=-1, keepdims=True)
    xc = x - mu
    var = jnp.mean(xc * xc, axis=-1, keepdims=True)
    return xc * lax.rsqrt(var + LN_EPS) * g + b


def _gelu_tanh(x):
    c = math.sqrt(2.0 / math.pi)
    return 0.5 * x * (1.0 + jnp.tanh(c * (x + 0.044715 * (x * x * x))))


def _sigmoid(x):
    return 1.0 / (1.0 + jnp.exp(-x))


def _resident(shape):
    nd = len(shape)
    return pl.BlockSpec(shape, lambda *_: (0,) * nd, pipeline_mode=pl.Buffered(1))


def _t5_bucket(dist, n_buckets):
    def run(ft):
        n = np.maximum(dist, 0)
        max_exact = n_buckets // 2
        nf = np.maximum(n, max_exact).astype(ft)
        large = max_exact + (np.log(nf / ft(max_exact)) / ft(math.log(MAX_DISTANCE / max_exact))
                             * ft(n_buckets - max_exact)).astype(np.int32)
        large = np.minimum(large, n_buckets - 1)
        return np.where(n < max_exact, n, large).astype(np.int32)
    b32, b64 = run(np.float32), run(np.float64)
    assert (b32 == b64).all(), "bucket boundaries are precision sensitive"
    return b32


def _const_bucket_distance(n_buckets):
    d = np.arange(0, 4 * MAX_DISTANCE)
    b = _t5_bucket(d, n_buckets)
    below = np.nonzero(b != n_buckets - 1)[0]
    return int(below.max()) + 1


def _bias_kernel(rb_ref, bkt_ref, out_ref, *, n_buckets):
    h = pl.program_id(0)
    bkt = bkt_ref[...]
    c_last = rb_ref[n_buckets - 1, h]
    acc = jnp.zeros(bkt.shape, F32)
    for b in range(n_buckets - 1):
        acc = jnp.where(bkt == b, (rb_ref[b, h] - c_last) * LOG2E, acc)
    out_ref[0] = jnp.where(bkt < 0, NEG_INF, acc)


def _bias_tables(rel_bias, bucket_np):
    n_buckets, n_heads = rel_bias.shape
    r, c = bucket_np.shape
    return pl.pallas_call(
        functools.partial(_bias_kernel, n_buckets=n_buckets),
        grid=(n_heads,),
        in_specs=[pl.BlockSpec(memory_space=pltpu.SMEM),
                  pl.BlockSpec((r, c), lambda h: (0, 0))],
        out_specs=pl.BlockSpec((1, r, c), lambda h: (h, 0, 0)),
        out_shape=jax.ShapeDtypeStruct((n_heads, r, c), F32),
        compiler_params=_cparams(("arbitrary",)),
        name="bias_tables",
    )(rel_bias, jnp.asarray(bucket_np))


def _inproj_kernel(x_ref, lng_ref, lnb_ref, w_ref, sg_ref, sb_ref, wmix_ref, bmix_ref,
                   q_ref, k4_ref, v4_ref, kb_ref, vb_ref, s_ref, *vn_refs, chunk, dw, sw, n_groups, q_scale):
    xn = _ln(x_ref[...], lng_ref[...], lnb_ref[...])
    xb = xn.astype(BF16)

    def proj(lo, n):
        return jnp.dot(xb, w_ref[:, lo:lo + n], preferred_element_type=F32)

    q_ref[...] = (proj(0, dw) * q_scale).astype(q_ref.dtype)
    n_heads, vd = k4_ref.shape[1:]
    for lo, r4, rb in ((dw, k4_ref, kb_ref), (2 * dw, v4_ref, vb_ref)):
        val = proj(lo, dw)
        rb[...] = val.astype(BF16)
        for h in range(n_heads):
            r4[:, h, :] = val[:, h * vd:(h + 1) * vd]
    u = _gelu_tanh(proj(3 * dw, sw))
    vn = _ln(_gelu_tanh(proj(3 * dw + sw, sw)), sg_ref[...], sb_ref[...])
    if vn_refs:
        vn_refs[0][...] = vn
    vb = vn.astype(BF16)
    tm = xb.shape[0]
    shift = chunk.bit_length() - 1
    row = lax.broadcasted_iota(I32, (tm, tm), 0)
    col = lax.broadcasted_iota(I32, (tm, tm), 1)
    causal = ((row >> shift) == (col >> shift)) & (col <= row)
    gw = sw // n_groups
    for g in range(n_groups):
        mg = jnp.where(causal, wmix_ref[g], 0.0).astype(BF16)
        sv = jnp.dot(mg, vb[:, g * gw:(g + 1) * gw], preferred_element_type=F32) + bmix_ref[:, g:g + 1]
        s_ref[:, g * gw:(g + 1) * gw] = (u[:, g * gw:(g + 1) * gw] * sv).astype(s_ref.dtype)


def _inproj(x2d, ln_g, ln_b, w_in_b, sgu_g, sgu_b_ln, sgu_w, sgu_bias, *, seq, tm, dw, sw, n_heads,
            q_scale, q_dtype, emit_vnorm):
    t, d = x2d.shape
    vd = dw // n_heads
    n_groups, chunk_full, _ = sgu_w.shape
    chunk = min(chunk_full, seq)
    assert chunk & (chunk - 1) == 0 and seq % chunk == 0 and tm % chunk == 0 and t % tm == 0
    rep = tm // chunk
    wmix = jnp.tile(sgu_w[:, :chunk, :chunk], (1, rep, rep))
    bmix = jnp.tile(sgu_bias[:, :chunk].T, (rep, 1))
    cols = w_in_b.shape[1]
    tok = lambda w: pl.BlockSpec((tm, w), lambda i: (i, 0))
    tok4 = pl.BlockSpec((tm, n_heads, vd), lambda i: (i, 0, 0))
    out_shape = [jax.ShapeDtypeStruct((t, dw), q_dtype), jax.ShapeDtypeStruct((t, n_heads, vd), F32),
                 jax.ShapeDtypeStruct((t, n_heads, vd), F32), jax.ShapeDtypeStruct((t, dw), BF16),
                 jax.ShapeDtypeStruct((t, dw), BF16), jax.ShapeDtypeStruct((t, sw), BF16)]
    out_specs = [tok(dw), tok4, tok4, tok(dw), tok(dw), tok(sw)]
    if emit_vnorm:
        out_shape.append(jax.ShapeDtypeStruct((t, sw), F32))
        out_specs.append(tok(sw))
    return pl.pallas_call(
        functools.partial(_inproj_kernel, chunk=chunk, dw=dw, sw=sw, n_groups=n_groups, q_scale=q_scale),
        grid=(t // tm,),
        in_specs=[tok(d), _resident((1, d)), _resident((1, d)), _resident((d, cols)),
                  _resident((1, sw)), _resident((1, sw)), _resident((n_groups, tm, tm)),
                  _resident((tm, n_groups))],
        out_specs=out_specs,
        out_shape=out_shape,
        compiler_params=_cparams(("arbitrary",)),
        name="inproj_sgu",
    )(x2d, ln_g.reshape(1, d), ln_b.reshape(1, d), w_in_b, sgu_g.reshape(1, sw), sgu_b_ln.reshape(1, sw),
      wmix, bmix)


def _diff_lambda_in_kernel(lam_ref, lam_init):
    lv = lam_ref[...]
    a = jnp.sum(lv[0:1] * lv[1:2], axis=-1, keepdims=True)
    b = jnp.sum(lv[2:3] * lv[3:4], axis=-1, keepdims=True)
    return jnp.exp(a) - jnp.exp(b) + lam_init


def _attn_prompt_kernel(q_ref, k_ref, v_ref, bias_ref, lam_ref, g_ref, o_ref, vt_sc, sa_sc, sb_sc, m_sc, acc_sc,
                        *, tq, hd, nh, lam_init):
    qi = pl.program_id(2)
    nk = vt_sc.shape[1]
    vd = 2 * hd
    heads = range(nh)
    hcols = lambda hh: slice(hh * vd, (hh + 1) * vd)

    @pl.when(qi == 0)
    def _():
        for hh in heads:
            for c in range(nk):
                vt_sc[hh, c, 0:vd, :] = v_ref[0, c * tq:(c + 1) * tq, hcols(hh)].astype(F32).T.astype(BF16)
                vt_sc[hh, c, vd:, :] = jnp.ones((ONES_ROWS, tq), BF16)

    sub = lax.broadcasted_iota(I32, (vd, tq), 0)
    qq = []
    for hh in heads:
        qt = q_ref[0, :, hcols(hh)].astype(F32).T
        qq.append(jnp.concatenate([jnp.where(sub < hd, qt, 0.0), jnp.where(sub >= hd, qt, 0.0)],
                                  axis=1).astype(BF16))

    def scores_to(s_sc, j):
        rows = pl.ds(pl.multiple_of(j * tq, tq), tq)
        for hh in heads:
            s_sc[hh] = jnp.dot(k_ref[0, rows, hcols(hh)], qq[hh], preferred_element_type=F32)

    def update_from(s_sc, j):
        d = qi - j
        kind = jnp.where(d < 0, 3, jnp.minimum(d, 2))
        j = jnp.minimum(j, nk - 1)
        for hh in heads:
            s = s_sc[hh] + bias_ref[hh, kind]
            m = m_sc[hh]
            m_new = jnp.maximum(m, jnp.max(s, axis=0, keepdims=True))
            p = jnp.exp2((s - m_new).astype(BF16))
            acc_sc[hh] = acc_sc[hh] * jnp.exp2(m - m_new) + jnp.dot(vt_sc[hh, j], p, preferred_element_type=F32)
            m_sc[hh] = m_new

    m_sc[...] = jnp.full(m_sc.shape, NEG_INF, F32)
    acc_sc[...] = jnp.zeros(acc_sc.shape, F32)
    scores_to(sa_sc, 0)

    def pair(i, c):
        j = 2 * i
        scores_to(sb_sc, jnp.minimum(j + 1, nk - 1))
        update_from(sa_sc, j)
        scores_to(sa_sc, jnp.minimum(j + 2, nk - 1))
        update_from(sb_sc, j + 1)
        return c

    lax.fori_loop(0, (qi + 2) // 2, pair, 0)
    lam = _diff_lambda_in_kernel(lam_ref, lam_init)
    for hh in heads:
        acc = acc_sc[hh]
        l = acc[vd:vd + 1, :]
        ot = acc[0:vd, :tq] / l[:, :tq] - lam * (acc[0:vd, tq:] / l[:, tq:])
        ms = jnp.mean(ot * ot, axis=0, keepdims=True)
        ot = ot * lax.rsqrt(ms + LN_EPS) * g_ref[...] * (1.0 - lam_init)
        o_ref[0, :, hcols(hh)] = ot.T.astype(o_ref.dtype)


def _attn_prompt(q, k, v, bias_p, lam_vecs, subln_g, *, n_heads, tq, nh, lam_init):
    b, s, dw = q.shape
    vd = dw // n_heads
    hd = vd // 2
    nq = s // tq
    assert n_heads % nh == 0
    return pl.pallas_call(
        functools.partial(_attn_prompt_kernel, tq=tq, hd=hd, nh=nh, lam_init=lam_init),
        grid=(b, n_heads // nh, nq),
        in_specs=[pl.BlockSpec((1, tq, nh * vd), lambda bi, h, qi: (bi, qi, h)),
                  pl.BlockSpec((1, s, nh * vd), lambda bi, h, qi: (bi, 0, h)),
                  pl.BlockSpec((1, s, nh * vd), lambda bi, h, qi: (bi, 0, h)),
                  pl.BlockSpec((nh, 4, tq, 2 * tq), lambda bi, h, qi: (h, 0, 0, 0)),
                  pl.BlockSpec((4, hd), lambda bi, h, qi: (0, 0)),
                  pl.BlockSpec((vd, 1), lambda bi, h, qi: (0, 0))],
        out_specs=pl.BlockSpec((1, tq, nh * vd), lambda bi, h, qi: (bi, qi, h)),
        out_shape=jax.ShapeDtypeStruct((b, s, dw), BF16),
        scratch_shapes=[pltpu.VMEM((nh, nq, vd + ONES_ROWS, tq), BF16), pltpu.VMEM((nh, tq, 2 * tq), F32),
                        pltpu.VMEM((nh, tq, 2 * tq), F32), pltpu.VMEM((nh, 1, 2 * tq), F32),
                        pltpu.VMEM((nh, vd + ONES_ROWS, 2 * tq), F32)],
        compiler_params=_cparams(("arbitrary", "arbitrary", "arbitrary")),
        name="attn_prompt",
    )(q, k, v, bias_p, lam_vecs, subln_g.reshape(vd, 1))


def _attn_decode_kernel(pt_ref, *refs, n_page_refs, n_heads, tq, hd, lam_init):
    k_refs = refs[:n_page_refs]
    v_refs = refs[n_page_refs:2 * n_page_refs]
    (q_ref, kn_ref, vn_ref, pen_ref, lam_ref, g_ref, o_ref, qz_sc, m_sc, l_sc, acc_sc) = refs[2 * n_page_refs:]
    del pt_ref
    step = pl.program_id(1)
    last = pl.num_programs(1) - 1
    vd = 2 * hd
    r2 = 2 * tq
    page = k_refs[0].shape[1] // n_heads
    nt_dims = (((1,), (1,)), ((), ()))

    @pl.when(step == 0)
    def _():
        lane = lax.broadcasted_iota(I32, (tq, vd), 1)
        for h in range(n_heads):
            qh = q_ref[:, h * vd:(h + 1) * vd]
            qz_sc[h] = jnp.concatenate([jnp.where(lane < hd, qh, 0.0), jnp.where(lane >= hd, qh, 0.0)],
                                       axis=0).astype(BF16)
        m_sc[...] = jnp.full(m_sc.shape, NEG_INF, F32)
        l_sc[...] = jnp.zeros(l_sc.shape, F32)
        acc_sc[...] = jnp.zeros(acc_sc.shape, F32)

    def update(head_blocks):
        all_s = []
        for h, blocks in enumerate(head_blocks):
            ss = []
            for kh, _, bias in blocks:
                s = lax.dot_general(qz_sc[h], kh.astype(BF16), nt_dims, preferred_element_type=F32)
                ss.append(s if bias is None else s + bias)
            all_s.append(ss)
        all_p, corrs = [], []
        for h, ss in enumerate(all_s):
            m = m_sc[h]
            m_new = jnp.maximum(m, jnp.max(functools.reduce(jnp.maximum, ss), axis=-1, keepdims=True))
            corr = jnp.exp2(m - m_new)
            ps = [jnp.exp2(s - m_new) for s in ss]
            l_sc[h] = l_sc[h] * corr + jnp.sum(functools.reduce(jnp.add, ps), axis=-1, keepdims=True)
            m_sc[h] = m_new
            all_p.append([p.astype(BF16) for p in ps])
            corrs.append(corr)
        for h, blocks in enumerate(head_blocks):
            acc = acc_sc[h] * corrs[h]
            for p, (_, vh, _) in zip(all_p[h], blocks):
                acc = acc + jnp.dot(p, vh.astype(BF16), preferred_element_type=F32)
            acc_sc[h] = acc

    is_last = (step == last).astype(F32)
    head_blocks = []
    for h in range(n_heads):
        rows = pl.ds(h, page, stride=n_heads)
        blocks = [(k_refs[i][0, rows, :], v_refs[i][0, rows, :], None) for i in range(n_page_refs - 1)]
        i = n_page_refs - 1
        blocks.append((k_refs[i][0, rows, :], v_refs[i][0, rows, :], pen_ref[0, h] * is_last))
        head_blocks.append(blocks)
    update(head_blocks)

    @pl.when(step == last)
    def _():
        pad = jnp.zeros((LANES - tq, vd), F32)
        lam = _diff_lambda_in_kernel(lam_ref, lam_init)
        head_blocks = []
        for h in range(n_heads):
            rows = pl.ds(h, tq, stride=n_heads)
            kh = jnp.concatenate([kn_ref[rows, :], pad], axis=0)
            vh = jnp.concatenate([vn_ref[rows, :], pad], axis=0)
            head_blocks.append([(kh, vh, pen_ref[1, h])])
        update(head_blocks)
        for h in range(n_heads):
            acc = acc_sc[h]
            l = l_sc[h]
            o = acc[0:tq] / l[0:tq] - lam * (acc[tq:r2] / l[tq:r2])
            ms = jnp.mean(o * o, axis=-1, keepdims=True)
            o_ref[:, h * vd:(h + 1) * vd] = (o * lax.rsqrt(ms + LN_EPS) * g_ref[...] * (1.0 - lam_init)
                                             ).astype(o_ref.dtype)


def _attn_decode(q, k_new, v_new, cache_k2, cache_v2, page_table, pen, lam_vecs, subln_g,
                 *, n_heads, tq, pages_per_step, lam_init):
    t, dw = q.shape
    bd, n_pages = page_table.shape
    _, page_rows, vd = cache_k2.shape
    hd = vd // 2
    pps = pages_per_step
    n_new = tq * n_heads
    assert n_pages % pps == 0 and tq % 8 == 0 and tq <= LANES <= page_rows // n_heads
    n_steps = n_pages // pps

    def page_spec(i):
        return pl.BlockSpec((1, page_rows, vd), lambda b, s, pt: (pt[b * n_pages + s * pps + i], 0, 0))

    tokb = pl.BlockSpec((tq, dw), lambda b, s, pt: (b, 0))
    newb = pl.BlockSpec((n_new, vd), lambda b, s, pt: (b, 0))
    full = lambda shape: pl.BlockSpec(shape, lambda b, s, pt: (0,) * len(shape))
    grid_spec = pltpu.PrefetchScalarGridSpec(
        num_scalar_prefetch=1,
        grid=(bd, n_steps),
        in_specs=[page_spec(i) for i in range(pps)] + [page_spec(i) for i in range(pps)]
        + [tokb, newb, newb, full(pen.shape), full((4, hd)), full((1, vd))],
        out_specs=tokb,
        scratch_shapes=[pltpu.VMEM((n_heads, 2 * tq, vd), BF16), pltpu.VMEM((n_heads, 2 * tq, 1), F32),
                        pltpu.VMEM((n_heads, 2 * tq, 1), F32), pltpu.VMEM((n_heads, 2 * tq, vd), F32)],
    )
    return pl.pallas_call(
        functools.partial(_attn_decode_kernel, n_page_refs=pps, n_heads=n_heads, tq=tq, hd=hd, lam_init=lam_init),
        grid_spec=grid_spec,
        out_shape=jax.ShapeDtypeStruct((t, dw), F32),
        compiler_params=_cparams(("arbitrary", "arbitrary")),
        name="attn_decode",
    )(page_table.reshape(-1), *([cache_k2] * pps), *([cache_v2] * pps), q, k_new, v_new, pen, lam_vecs,
      subln_g.reshape(1, vd))


def _outproj_kernel(x_ref, att_ref, sgu_ref, lng_ref, lnb_ref, wo_ref, l1g_ref, l1b_ref, wr_ref, br_ref,
                    h1_ref, eidx_ref, topw_ref, cnt_ref, *, alpha, dw, n_experts, n_groups, n_sub):
    i = pl.program_id(0)
    tm = x_ref.shape[0]
    ts = tm // n_sub
    epg = n_experts // n_groups
    lane = lax.broadcasted_iota(I32, (ts, LANES), 1).astype(F32)
    big = float(LANES)
    ninf = -jnp.inf
    is_g = (lane >= n_experts) & (lane < n_experts + n_groups)
    subs = [slice(c * ts, (c + 1) * ts) for c in range(n_sub)]
    mixes = [jnp.dot(att_ref[r, :].astype(BF16), wo_ref[0:dw, :], preferred_element_type=F32)
             + jnp.dot(sgu_ref[r, :].astype(BF16), wo_ref[dw:, :], preferred_element_type=F32) for r in subs]
    h1s = []
    for r, mix in zip(subs, mixes):
        hp = _ln(x_ref[r, :], lng_ref[...], lnb_ref[...])
        h1 = _ln(alpha * hp + mix, l1g_ref[...], l1b_ref[...])
        h1_ref[r, :] = h1
        h1s.append(h1.astype(BF16))
    all_logits = [jnp.dot(h1b, wr_ref[...], preferred_element_type=F32) + br_ref[...] for h1b in h1s]
    cnt = jnp.zeros((1, LANES), F32)
    for r, logits in zip(subs, all_logits):
        gmax = jnp.max(jnp.where(is_g, logits, ninf), axis=-1, keepdims=True)
        gidx = jnp.min(jnp.where(is_g & (logits == gmax), lane, big), axis=-1, keepdims=True) - n_experts
        gsum = jnp.sum(jnp.where(is_g, jnp.exp(logits - gmax), 0.0), axis=-1, keepdims=True)
        g_w = 1.0 / gsum
        sel = (lane >= gidx * epg) & (lane < (gidx + 1.0) * epg)
        v1 = jnp.max(jnp.where(sel, logits, ninf), axis=-1, keepdims=True)
        i1 = jnp.min(jnp.where(sel & (logits == v1), lane, big), axis=-1, keepdims=True)
        sel2 = sel & (lane != i1)
        v2 = jnp.max(jnp.where(sel2, logits, ninf), axis=-1, keepdims=True)
        i2 = jnp.min(jnp.where(sel2 & (logits == v2), lane, big), axis=-1, keepdims=True)
        e = jnp.exp(v2 - v1)
        eidx_ref[r, 0:1] = i1.astype(I32)
        eidx_ref[r, 1:2] = i2.astype(I32)
        topw_ref[r, 0:1] = g_w / (1.0 + e)
        topw_ref[r, 1:2] = g_w * e / (1.0 + e)
        onehot = ((lane == i1) | (lane == i2)).astype(F32)
        cnt = cnt + jnp.sum(onehot, axis=0, keepdims=True)

    @pl.when(i == 0)
    def _():
        cnt_ref[...] = jnp.zeros(cnt_ref.shape, F32)

    cnt_ref[...] += jnp.broadcast_to(cnt, cnt_ref.shape)


def _outproj_router(x2d, att, sgu, ln_g, ln_b, w_o_b, l1g, l1b, w_r_b, b_r, *, tm, alpha, n_experts, n_groups):
    t, d = x2d.shape
    n_sub = 2 if t % (2 * tm) == 0 else 1
    tm = n_sub * tm
    dw = att.shape[1]
    sw = sgu.shape[1]
    tok = lambda w: pl.BlockSpec((tm, w), lambda i: (i, 0))
    return pl.pallas_call(
        functools.partial(_outproj_kernel, alpha=alpha, dw=dw, n_experts=n_experts, n_groups=n_groups,
                          n_sub=n_sub),
        grid=(t // tm,),
        in_specs=[tok(d), tok(dw), tok(sw), _resident((1, d)), _resident((1, d)), _resident((dw + sw, d)),
                  _resident((1, d)), _resident((1, d)), _resident((d, LANES)), _resident((1, LANES))],
        out_specs=[tok(d), tok(TOP_K), tok(TOP_K), pl.BlockSpec((8, LANES), lambda i: (0, 0))],
        out_shape=[jax.ShapeDtypeStruct((t, d), F32), jax.ShapeDtypeStruct((t, TOP_K), I32),
                   jax.ShapeDtypeStruct((t, TOP_K), F32), jax.ShapeDtypeStruct((8, LANES), F32)],
        compiler_params=_cparams(("arbitrary",)),
        name="outproj_router",
    )(x2d, att, sgu, ln_g.reshape(1, d), ln_b.reshape(1, d), w_o_b, l1g.reshape(1, d), l1b.reshape(1, d),
      w_r_b, b_r)


def _pos_kernel(eidx_ref, base_ref, pos_ref, run_sc):
    i = pl.program_id(0)

    @pl.when(i == 0)
    def _():
        run_sc[...] = jnp.zeros(run_sc.shape, F32)

    tm = eidx_ref.shape[0]
    lane = lax.broadcasted_iota(I32, (tm, LANES), 1)
    oh0 = (lane == eidx_ref[:, 0:1]).astype(F32)
    oh1 = (lane == eidx_ref[:, 1:2]).astype(F32)
    oh = oh0 + oh1
    row = lax.broadcasted_iota(I32, (tm, tm), 0)
    col = lax.broadcasted_iota(I32, (tm, tm), 1)
    lower = (col < row).astype(BF16)
    rank = jnp.dot(lower, oh.astype(BF16), preferred_element_type=F32)
    posmat = rank + base_ref[...] + run_sc[...]
    for k, ohk in enumerate((oh0, oh1)):
        pk = jnp.sum(ohk * posmat, axis=-1, keepdims=True)
        pk_rows = jnp.broadcast_to(pk, (tm, LANES)).T
        pos_ref[0, k:k + 1, :] = pk_rows[0:1, :].astype(I32)
    run_sc[...] += jnp.sum(oh, axis=0, keepdims=True)


def _positions(eidx, base, *, tm):
    t = eidx.shape[0]
    nt = t // tm
    return pl.pallas_call(
        _pos_kernel,
        grid=(nt,),
        in_specs=[pl.BlockSpec((tm, TOP_K), lambda i: (i, 0)), pl.BlockSpec((1, LANES), lambda i: (0, 0))],
        out_specs=pl.BlockSpec((1, TOP_K, tm), lambda i: (i, 0, 0)),
        out_shape=jax.ShapeDtypeStruct((nt, TOP_K, tm), I32),
        scratch_shapes=[pltpu.VMEM((1, LANES), F32)],
        compiler_params=_cparams(("arbitrary",)),
        name="route_positions",
    )(eidx, base)


def _scatter_kernel(pos_ref, h1_ref, xs_hbm, sem, *, tm):
    def row_copy(t, k):
        return pltpu.make_async_copy(h1_ref.at[pl.ds(t, 1)], xs_hbm.at[pl.ds(pos_ref[0, k, t], 1)], sem)

    def issue(t, c):
        for k in range(TOP_K):
            row_copy(t, k).start()
        return c

    def drain(t, c):
        for k in range(TOP_K):
            row_copy(t, k).wait()
        return c

    lax.fori_loop(0, tm, issue, 0, unroll=8)
    lax.fori_loop(0, tm, drain, 0, unroll=8)


def _scatter_rows(h1, pos, *, tm):
    t, d = h1.shape
    nt = t // tm
    return pl.pallas_call(
        functools.partial(_scatter_kernel, tm=tm),
        grid=(nt,),
        in_specs=[pl.BlockSpec((1, TOP_K, tm), lambda i: (i, 0, 0), memory_space=pltpu.SMEM),
                  pl.BlockSpec((tm, d), lambda i: (i, 0))],
        out_specs=pl.BlockSpec(memory_space=pl.ANY),
        out_shape=jax.ShapeDtypeStruct((TOP_K * t, d), h1.dtype),
        scratch_shapes=[pltpu.SemaphoreType.DMA(())],
        compiler_params=_cparams(("arbitrary",)),
        name="scatter_to_experts",
    )(pos, h1)


def _expert_kernel(tile_ref, grp_ref, lo_ref, hi_ref, first_ref, x_ref, wg_ref, wu_ref, wd_ref, o_ref,
                   wgb, wub, wdb, *, tm):
    w = pl.program_id(0)
    changed = (w == 0) | (grp_ref[w] != grp_ref[jnp.maximum(w - 1, 0)])

    @pl.when(changed)
    def _():
        wgb[...] = wg_ref[0].astype(BF16)
        wub[...] = wu_ref[0].astype(BF16)
        wdb[...] = wd_ref[0].astype(BF16)

    xb = x_ref[...].astype(BF16)
    g = jnp.dot(xb, wgb[...], preferred_element_type=F32)
    u = jnp.dot(xb, wub[...], preferred_element_type=F32)
    hid = (g * _sigmoid(g) * u).astype(BF16)
    o = jnp.dot(hid, wdb[...], preferred_element_type=F32)
    row = tile_ref[w] * tm + lax.broadcasted_iota(I32, (tm, 1), 0)
    o = jnp.where((row >= lo_ref[w]) & (row < hi_ref[w]), o, 0.0)

    @pl.when(first_ref[w] == 1)
    def _():
        o_ref[...] = o

    @pl.when(first_ref[w] == 0)
    def _():
        o_ref[...] += o


def _expert_ffn(xs, w_gate, w_up, w_down, meta, *, tm):
    a, d = xs.shape
    n_experts, _, de = w_gate.shape
    tile_ids, grp_ids, lo, hi, first = meta
    n_work = tile_ids.shape[0]
    grid_spec = pltpu.PrefetchScalarGridSpec(
        num_scalar_prefetch=5,
        grid=(n_work,),
        in_specs=[pl.BlockSpec((tm, d), lambda w, ti, gi, lo_, hi_, fi: (ti[w], 0)),
                  pl.BlockSpec((1, d, de), lambda w, ti, gi, lo_, hi_, fi: (gi[w], 0, 0)),
                  pl.BlockSpec((1, d, de), lambda w, ti, gi, lo_, hi_, fi: (gi[w], 0, 0)),
                  pl.BlockSpec((1, de, d), lambda w, ti, gi, lo_, hi_, fi: (gi[w], 0, 0))],
        out_specs=pl.BlockSpec((tm, d), lambda w, ti, gi, lo_, hi_, fi: (ti[w], 0)),
        scratch_shapes=[pltpu.VMEM((d, de), BF16), pltpu.VMEM((d, de), BF16), pltpu.VMEM((de, d), BF16)],
    )
    return pl.pallas_call(
        functools.partial(_expert_kernel, tm=tm),
        grid_spec=grid_spec,
        out_shape=jax.ShapeDtypeStruct((a, d), F32),
        compiler_params=_cparams(("arbitrary",)),
        name="expert_ffn",
    )(tile_ids, grp_ids, lo, hi, first, xs, w_gate, w_up, w_down)


def _group_metadata(counts, *, n_rows, tm):
    n_experts = counts.shape[0]
    nt = n_rows // tm
    n_work = nt + n_experts - 1
    ends = jnp.cumsum(counts)
    starts = ends - counts
    ntiles_g = jnp.where(counts > 0, (ends - 1) // tm - starts // tm + 1, 0)
    work_end = jnp.cumsum(ntiles_g)
    work_start = work_end - ntiles_g
    total = work_end[-1]
    w = jnp.arange(n_work, dtype=I32)
    wc = jnp.minimum(w, total - 1)
    g = jnp.minimum(jnp.sum((work_end[None, :] <= wc[:, None]).astype(I32), axis=1), n_experts - 1)
    onehot = (g[:, None] == jnp.arange(n_experts, dtype=I32)[None, :]).astype(I32)
    pick = lambda a: jnp.sum(onehot * a[None, :], axis=1)
    starts_g, ends_g = pick(starts), pick(ends)
    tile = (starts_g // tm + (wc - pick(work_start))).astype(I32)
    valid = w < total
    lo = jnp.where(valid, jnp.maximum(starts_g, tile * tm), 0).astype(I32)
    hi = jnp.where(valid, jnp.minimum(ends_g, (tile + 1) * tm), 0).astype(I32)
    prev_tile = jnp.concatenate([jnp.full((1,), -1, I32), tile[:-1]])
    first = (tile != prev_tile).astype(I32)
    return tile, g, lo, hi, first


def _final_kernel(pos_ref, posn_ref, h1_ref, topw_ref, p_ref, os_hbm, l2g_ref, l2b_ref, wpg_ref, bpg_ref,
                  wpp_ref, y_ref, buf, sem, *, tm, alpha):
    i = pl.program_id(0)
    n = pl.num_programs(0)
    slot = i % 2

    def row_copy(pref, sl, t, k):
        return pltpu.make_async_copy(os_hbm.at[pl.ds(pref[0, k, t], 1)], buf.at[sl, k, pl.ds(t, 1)], sem.at[sl])

    def issue(pref, sl):
        def body(t, c):
            for k in range(TOP_K):
                row_copy(pref, sl, t, k).start()
            return c
        lax.fori_loop(0, tm, body, 0, unroll=8)

    @pl.when(i == 0)
    def _():
        issue(pos_ref, 0)

    @pl.when(i + 1 < n)
    def _():
        issue(posn_ref, 1 - slot)

    def drain(t, c):
        for k in range(TOP_K):
            row_copy(pos_ref, slot, t, k).wait()
        return c

    lax.fori_loop(0, tm, drain, 0, unroll=8)
    y = topw_ref[:, 0:1] * buf[slot, 0] + topw_ref[:, 1:2] * buf[slot, 1]
    h2 = _ln(alpha * h1_ref[...] + y, l2g_ref[...], l2b_ref[...])
    gate = _sigmoid(jnp.dot(h2.astype(BF16), wpg_ref[...], preferred_element_type=F32) + bpg_ref[...])
    y_ref[...] = h2 + gate * jnp.dot(p_ref[...].astype(BF16), wpp_ref[...], preferred_element_type=F32)


def _combine_ple(h1, topw, pos, p2d, out_sorted, l2g, l2b, w_pg_b, b_pg, w_pp_b, *, tm, alpha):
    t, d = h1.shape
    nt = t // tm
    ple = p2d.shape[1]
    tok = lambda w: pl.BlockSpec((tm, w), lambda i: (i, 0))
    return pl.pallas_call(
        functools.partial(_final_kernel, tm=tm, alpha=alpha),
        grid=(nt,),
        in_specs=[pl.BlockSpec((1, TOP_K, tm), lambda i: (i, 0, 0), memory_space=pltpu.SMEM),
                  pl.BlockSpec((1, TOP_K, tm), lambda i: (jnp.minimum(i + 1, nt - 1), 0, 0),
                               memory_space=pltpu.SMEM),
                  tok(d), tok(TOP_K), tok(ple), pl.BlockSpec(memory_space=pl.ANY),
                  _resident((1, d)), _resident((1, d)), _resident((d, d)), _resident((1, d)),
                  _resident((ple, d))],
        out_specs=tok(d),
        out_shape=jax.ShapeDtypeStruct((t, d), F32),
        scratch_shapes=[pltpu.VMEM((2, TOP_K, tm, d), F32), pltpu.SemaphoreType.DMA((2,))],
        compiler_params=_cparams(("arbitrary",)),
        name="combine_ple",
    )(pos, pos, h1, topw, p2d, out_sorted, l2g.reshape(1, d), l2b.reshape(1, d), w_pg_b, b_pg.reshape(1, d),
      w_pp_b)


def _post_mix(x2d, att, sgu, p2d, prm, *, tm, alpha, n_experts, n_groups):
    h1, eidx, topw, cnt = _outproj_router(
        x2d, att, sgu, prm["ln_emb_g"], prm["ln_emb_b"], prm["w_o"], prm["ln1_g"], prm["ln1_b"],
        prm["w_r"], prm["b_r"], tm=tm, alpha=alpha, n_experts=n_experts, n_groups=n_groups)
    t = x2d.shape[0]
    counts = cnt[0, :n_experts].astype(I32)
    starts = jnp.cumsum(counts) - counts
    base = jnp.pad(starts.astype(F32), (0, LANES - n_experts)).reshape(1, LANES)
    pos = _positions(eidx, base, tm=tm)
    xs = _scatter_rows(h1, pos, tm=tm)
    meta = _group_metadata(counts, n_rows=TOP_K * t, tm=tm)
    out_sorted = _expert_ffn(xs, prm["w_gate"], prm["w_up"], prm["w_down"], meta, tm=tm)
    return _combine_ple(h1, topw, pos, p2d, out_sorted, prm["ln2_g"], prm["ln2_b"], prm["w_pg"], prm["b_pg"],
                        prm["w_pp"], tm=tm, alpha=alpha)


def kernel(x_prompt, x_sample, cache_k, cache_v, page_table, p_prompt, p_sample, ln_emb_g, ln_emb_b, w_in, lambda_q1, lambda_k1, lambda_q2, lambda_k2, subln_g, rel_bias, sgu_ln_g, sgu_ln_b, sgu_w, sgu_b, w_o, ln1_g, ln1_b, w_router_group, b_router_group, w_router_expert, b_router_expert, w_gate, w_up, w_down, ln2_g, ln2_b, w_ple_gate, b_ple_gate, w_ple_proj):
    bsz, seq, d = x_prompt.shape
    bd, tq_dec, _ = x_sample.shape
    depth, _, page, n_heads, vd = cache_v.shape
    hd = vd // 2
    dw = n_heads * vd
    sw = (w_in.shape[-1] - 3 * dw) // 2
    n_groups = w_router_group.shape[-1]
    n_experts = w_router_expert.shape[-1]
    n_buckets = rel_bias.shape[0]
    ple = p_prompt.shape[-1]
    alpha = (2.0 * depth) ** 0.25
    assert n_experts + n_groups <= LANES

    tp, ts = bsz * seq, bd * tq_dec
    tm_p = min(256, tp)
    tm_s = min(256, ts)
    tq = min(256, seq)
    n_pages = page_table.shape[1]
    pps = math.gcd(n_pages, 8)

    d_const = _const_bucket_distance(n_buckets)
    assert tq + 1 >= d_const and page + 1 >= d_const
    qi = np.arange(tq)[:, None]
    kj = np.arange(tq)[None, :]
    bkt_p = np.concatenate([np.where(qi <= kj, _t5_bucket(kj - qi, n_buckets), -1).astype(np.int32),
                            _t5_bucket(tq + kj - qi, n_buckets), np.full((tq, tq), n_buckets - 1, np.int32),
                            np.full((tq, tq), -1, np.int32)], axis=0)
    bias_p = _bias_tables(rel_bias, np.tile(bkt_p, (1, 2))).reshape(n_heads, 4, tq, 2 * tq)
    r2 = 2 * tq_dec
    di = np.tile(np.arange(tq_dec), 2)[:, None]
    dj = np.arange(page)[None, :]
    bkt_last = _t5_bucket(page + di - dj, n_buckets)
    bkt_new = np.where((dj <= di) & (dj < tq_dec), _t5_bucket(di - dj, n_buckets), -1).astype(np.int32)
    pen = _bias_tables(rel_bias, np.concatenate([bkt_last, bkt_new], axis=0))
    pen = pen.reshape(n_heads, 2, r2, page).transpose(1, 0, 2, 3)

    xp = x_prompt.reshape(tp, d)
    xs = x_sample.reshape(ts, d)
    hp_x, hs_x = xp, xs
    k_p_rows, v_p_rows, k_s_rows, v_s_rows, sgu_s_rows = [], [], [], [], []
    for l in range(depth):
        assert depth == 1, "the trunk input of deeper layers is the previous layer's output"
        lam_init = 0.8 - 0.6 * math.exp(-0.3 * l)
        lam_vecs = jnp.stack([lambda_q1[l], lambda_k1[l], lambda_q2[l], lambda_k2[l]])
        w_in_b = w_in[l].astype(BF16)
        pad_r = LANES - n_experts - n_groups
        prm = dict(
            ln_emb_g=ln_emb_g, ln_emb_b=ln_emb_b, w_o=w_o[l].astype(BF16), ln1_g=ln1_g[l], ln1_b=ln1_b[l],
            w_r=jnp.pad(jnp.concatenate([w_router_expert[l], w_router_group[l]], axis=1),
                        ((0, 0), (0, pad_r))).astype(BF16),
            b_r=jnp.pad(jnp.concatenate([b_router_expert[l], b_router_group[l]]), (0, pad_r)).reshape(1, LANES),
            w_gate=w_gate[l], w_up=w_up[l], w_down=w_down[l], ln2_g=ln2_g[l], ln2_b=ln2_b[l],
            w_pg=w_ple_gate[l].astype(BF16), b_pg=b_ple_gate[l], w_pp=w_ple_proj[l].astype(BF16))
        common = dict(dw=dw, sw=sw, n_heads=n_heads, q_scale=hd ** -0.5 * LOG2E)
        qp, kp, vp, kbp, vbp, sgu_p = _inproj(
            hp_x, ln_emb_g, ln_emb_b, w_in_b, sgu_ln_g[l], sgu_ln_b[l], sgu_w[l], sgu_b[l],
            seq=seq, tm=tm_p, q_dtype=BF16, emit_vnorm=False, **common)
        qs, ks, vs, _, _, sgu_s, vn_s = _inproj(
            hs_x, ln_emb_g, ln_emb_b, w_in_b, sgu_ln_g[l], sgu_ln_b[l], sgu_w[l], sgu_b[l],
            seq=tq_dec, tm=tm_s, q_dtype=F32, emit_vnorm=True, **common)
        att_p = _attn_prompt(qp.reshape(bsz, seq, dw), kbp.reshape(bsz, seq, dw), vbp.reshape(bsz, seq, dw),
                             bias_p, lam_vecs, subln_g[l], n_heads=n_heads, tq=tq, nh=math.gcd(n_heads, 4),
                             lam_init=lam_init)
        ck = (cache_k if depth == 1 else cache_k[l]).reshape(-1, page * n_heads, vd)
        cv = (cache_v if depth == 1 else cache_v[l]).reshape(-1, page * n_heads, vd)
        att_s = _attn_decode(qs, ks.reshape(-1, vd), vs.reshape(-1, vd), ck, cv, page_table, pen, lam_vecs,
                             subln_g[l], n_heads=n_heads, tq=tq_dec, pages_per_step=pps, lam_init=lam_init)
        post = dict(alpha=alpha, n_experts=n_experts, n_groups=n_groups)
        hp_x = _post_mix(hp_x, att_p.reshape(tp, dw), sgu_p, p_prompt[l].reshape(tp, ple), prm, tm=tm_p, **post)
        hs_x = _post_mix(hs_x, att_s, sgu_s, p_sample[l].reshape(ts, ple), prm, tm=tm_s, **post)
        k_p_rows.append(kp.reshape(bsz, seq, n_heads, vd))
        v_p_rows.append(vp.reshape(bsz, seq, n_heads, vd))
        k_s_rows.append(ks.reshape(bd, tq_dec, n_heads, vd))
        v_s_rows.append(vs.reshape(bd, tq_dec, n_heads, vd))
        sgu_s_rows.append(vn_s.reshape(bd, tq_dec, sw))
    stack = lambda rows: rows[0][None] if len(rows) == 1 else jnp.stack(rows)
    return (hp_x.reshape(bsz, seq, d), hs_x.reshape(bd, tq_dec, d), stack(k_p_rows), stack(v_p_rows),
            stack(k_s_rows), stack(v_s_rows), stack(sgu_s_rows))
```

```python
import functools
import math

import numpy as np
import jax
import jax.numpy as jnp
from jax import lax
from jax.experimental import pallas as pl
from jax.experimental.pallas import tpu as pltpu

F32 = jnp.float32
BF16 = jnp.bfloat16
I32 = jnp.int32

LN_EPS = 1e-5
NEG_INF = -1e30
MAX_DISTANCE = 128
TOP_K = 2
LANES = 128
ONES_ROWS = 16
LOG2E = math.log2(math.e)
VMEM_LIMIT = 56 * 1024 * 1024


def _cparams(sem, vmem=VMEM_LIMIT):
    return pltpu.CompilerParams(dimension_semantics=sem, vmem_limit_bytes=vmem)


def _ln(x, g, b):
    mu = jnp.mean(x, axis=-1, keepdims=True)
    xc = x - mu
    var = jnp.mean(xc * xc, axis=-1, keepdims=True)
    return xc * lax.rsqrt(var + LN_EPS) * g + b


def _gelu_tanh(x):
    c = math.sqrt(2.0 / math.pi)
    return 0.5 * x * (1.0 + jnp.tanh(c * (x + 0.044715 * (x * x * x))))


def _sigmoid(x):
    return 1.0 / (1.0 + jnp.exp(-x))


def _resident(shape):
    nd = len(shape)
    return pl.BlockSpec(shape, lambda *_: (0,) * nd, pipeline_mode=pl.Buffered(1))


def _t5_bucket(dist, n_buckets):
    def run(ft):
        n = np.maximum(dist, 0)
        max_exact = n_buckets // 2
        nf = np.maximum(n, max_exact).astype(ft)
        large = max_exact + (np.log(nf / ft(max_exact)) / ft(math.log(MAX_DISTANCE / max_exact))
                             * ft(n_buckets - max_exact)).astype(np.int32)
        large = np.minimum(large, n_buckets - 1)
        return np.where(n < max_exact, n, large).astype(np.int32)
    b32, b64 = run(np.float32), run(np.float64)
    assert (b32 == b64).all(), "bucket boundaries are precision sensitive"
    return b32


def _const_bucket_distance(n_buckets):
    d = np.arange(0, 4 * MAX_DISTANCE)
    b = _t5_bucket(d, n_buckets)
    below = np.nonzero(b != n_buckets - 1)[0]
    return int(below.max()) + 1


def _bias_kernel(rb_ref, bkt_ref, out_ref, *, n_buckets):
    h = pl.program_id(0)
    bkt = bkt_ref[...]
    c_last = rb_ref[n_buckets - 1, h]
    acc = jnp.zeros(bkt.shape, F32)
    for b in range(n_buckets - 1):
        acc = jnp.where(bkt == b, (rb_ref[b, h] - c_last) * LOG2E, acc)
    out_ref[0] = jnp.where(bkt < 0, NEG_INF, acc)


def _bias_tables(rel_bias, bucket_np):
    n_buckets, n_heads = rel_bias.shape
    r, c = bucket_np.shape
    return pl.pallas_call(
        functools.partial(_bias_kernel, n_buckets=n_buckets),
        grid=(n_heads,),
        in_specs=[pl.BlockSpec(memory_space=pltpu.SMEM),
                  pl.BlockSpec((r, c), lambda h: (0, 0))],
        out_specs=pl.BlockSpec((1, r, c), lambda h: (h, 0, 0)),
        out_shape=jax.ShapeDtypeStruct((n_heads, r, c), F32),
        compiler_params=_cparams(("arbitrary",)),
        name="bias_tables",
    )(rel_bias, jnp.asarray(bucket_np))


def _inproj_kernel(x_ref, lng_ref, lnb_ref, w_ref, sg_ref, sb_ref, wmix_ref, bmix_ref,
                   q_ref, k4_ref, v4_ref, kb_ref, vb_ref, s_ref, *vn_refs, chunk, dw, sw, n_groups, q_scale):
    xn = _ln(x_ref[...], lng_ref[...], lnb_ref[...])
    xb = xn.astype(BF16)

    def proj(lo, n):
        return jnp.dot(xb, w_ref[:, lo:lo + n], preferred_element_type=F32)

    zu = proj(3 * dw, sw)
    zg = proj(3 * dw + sw, sw)
    q_ref[...] = (proj(0, dw) * q_scale).astype(q_ref.dtype)
    n_heads, vd = k4_ref.shape[1:]
    for lo, r4, rb in ((dw, k4_ref, kb_ref), (2 * dw, v4_ref, vb_ref)):
        val = proj(lo, dw)
        rb[...] = val.astype(BF16)
        for h in range(n_heads):
            r4[:, h, :] = val[:, h * vd:(h + 1) * vd]
    u = _gelu_tanh(zu)
    vn = _ln(_gelu_tanh(zg), sg_ref[...], sb_ref[...])
    if vn_refs:
        vn_refs[0][...] = vn
    vb = vn.astype(BF16)
    tm = xb.shape[0]
    shift = chunk.bit_length() - 1
    row = lax.broadcasted_iota(I32, (tm, tm), 0)
    col = lax.broadcasted_iota(I32, (tm, tm), 1)
    causal = ((row >> shift) == (col >> shift)) & (col <= row)
    gw = sw // n_groups
    for g in range(n_groups):
        mg = jnp.where(causal, wmix_ref[g], 0.0).astype(BF16)
        sv = jnp.dot(mg, vb[:, g * gw:(g + 1) * gw], preferred_element_type=F32) + bmix_ref[:, g:g + 1]
        s_ref[:, g * gw:(g + 1) * gw] = (u[:, g * gw:(g + 1) * gw] * sv).astype(s_ref.dtype)


def _inproj(x2d, ln_g, ln_b, w_in_b, sgu_g, sgu_b_ln, sgu_w, sgu_bias, *, seq, tm, dw, sw, n_heads,
            q_scale, q_dtype, emit_vnorm):
    t, d = x2d.shape
    vd = dw // n_heads
    n_groups, chunk_full, _ = sgu_w.shape
    chunk = min(chunk_full, seq)
    assert chunk & (chunk - 1) == 0 and seq % chunk == 0 and tm % chunk == 0 and t % tm == 0
    rep = tm // chunk
    wmix = jnp.tile(sgu_w[:, :chunk, :chunk], (1, rep, rep))
    bmix = jnp.tile(sgu_bias[:, :chunk].T, (rep, 1))
    cols = w_in_b.shape[1]
    tok = lambda w: pl.BlockSpec((tm, w), lambda i: (i, 0))
    tok4 = pl.BlockSpec((tm, n_heads, vd), lambda i: (i, 0, 0))
    out_shape = [jax.ShapeDtypeStruct((t, dw), q_dtype), jax.ShapeDtypeStruct((t, n_heads, vd), F32),
                 jax.ShapeDtypeStruct((t, n_heads, vd), F32), jax.ShapeDtypeStruct((t, dw), BF16),
                 jax.ShapeDtypeStruct((t, dw), BF16), jax.ShapeDtypeStruct((t, sw), BF16)]
    out_specs = [tok(dw), tok4, tok4, tok(dw), tok(dw), tok(sw)]
    if emit_vnorm:
        out_shape.append(jax.ShapeDtypeStruct((t, sw), F32))
        out_specs.append(tok(sw))
    return pl.pallas_call(
        functools.partial(_inproj_kernel, chunk=chunk, dw=dw, sw=sw, n_groups=n_groups, q_scale=q_scale),
        grid=(t // tm,),
        in_specs=[tok(d), _resident((1, d)), _resident((1, d)), _resident((d, cols)),
                  _resident((1, sw)), _resident((1, sw)), _resident((n_groups, tm, tm)),
                  _resident((tm, n_groups))],
        out_specs=out_specs,
        out_shape=out_shape,
        compiler_params=_cparams(("arbitrary",)),
        name="inproj_sgu",
    )(x2d, ln_g.reshape(1, d), ln_b.reshape(1, d), w_in_b, sgu_g.reshape(1, sw), sgu_b_ln.reshape(1, sw),
      wmix, bmix)


def _diff_lambda_in_kernel(lam_ref, lam_init):
    lv = lam_ref[...]
    a = jnp.sum(lv[0:1] * lv[1:2], axis=-1, keepdims=True)
    b = jnp.sum(lv[2:3] * lv[3:4], axis=-1, keepdims=True)
    return jnp.exp(a) - jnp.exp(b) + lam_init


def _attn_prompt_kernel(q_ref, k_ref, v_ref, bias_ref, lam_ref, g_ref, o_ref, vt_sc, sa_sc, sb_sc, m_sc, acc_sc,
                        *, tq, hd, nh, lam_init):
    qi = pl.program_id(2)
    nk = vt_sc.shape[1]
    vd = 2 * hd
    heads = range(nh)
    hcols = lambda hh: slice(hh * vd, (hh + 1) * vd)

    @pl.when(qi == 0)
    def _():
        for hh in heads:
            for c in range(nk):
                vt_sc[hh, c, 0:vd, :] = v_ref[0, c * tq:(c + 1) * tq, hcols(hh)].astype(F32).T.astype(BF16)
                vt_sc[hh, c, vd:, :] = jnp.ones((ONES_ROWS, tq), BF16)

    sub = lax.broadcasted_iota(I32, (vd, tq), 0)
    qq = []
    for hh in heads:
        qt = q_ref[0, :, hcols(hh)].astype(F32).T
        qq.append(jnp.concatenate([jnp.where(sub < hd, qt, 0.0), jnp.where(sub >= hd, qt, 0.0)],
                                  axis=1).astype(BF16))

    def scores_to(s_sc, j):
        rows = pl.ds(pl.multiple_of(j * tq, tq), tq)
        for hh in heads:
            s_sc[hh] = jnp.dot(k_ref[0, rows, hcols(hh)], qq[hh], preferred_element_type=F32)

    def update_from(s_sc, j):
        d = qi - j
        kind = jnp.where(d < 0, 3, jnp.minimum(d, 2))
        j = jnp.minimum(j, nk - 1)
        for hh in heads:
            s = s_sc[hh] + bias_ref[hh, kind]
            m = m_sc[hh]
            m_new = jnp.maximum(m, jnp.max(s, axis=0, keepdims=True))
            p = jnp.exp2((s - m_new).astype(BF16))
            acc_sc[hh] = acc_sc[hh] * jnp.exp2(m - m_new) + jnp.dot(vt_sc[hh, j], p, preferred_element_type=F32)
            m_sc[hh] = m_new

    m_sc[...] = jnp.full(m_sc.shape, NEG_INF, F32)
    acc_sc[...] = jnp.zeros(acc_sc.shape, F32)
    scores_to(sa_sc, 0)

    def pair(i, c):
        j = 2 * i
        scores_to(sb_sc, jnp.minimum(j + 1, nk - 1))
        update_from(sa_sc, j)
        scores_to(sa_sc, jnp.minimum(j + 2, nk - 1))
        update_from(sb_sc, j + 1)
        return c

    lax.fori_loop(0, (qi + 2) // 2, pair, 0)
    lam = _diff_lambda_in_kernel(lam_ref, lam_init)
    for hh in heads:
        acc = acc_sc[hh]
        l = acc[vd:vd + 1, :]
        ot = acc[0:vd, :tq] / l[:, :tq] - lam * (acc[0:vd, tq:] / l[:, tq:])
        ms = jnp.mean(ot * ot, axis=0, keepdims=True)
        ot = ot * lax.rsqrt(ms + LN_EPS) * g_ref[...] * (1.0 - lam_init)
        o_ref[0, :, hcols(hh)] = ot.T.astype(o_ref.dtype)


def _attn_prompt(q, k, v, bias_p, lam_vecs, subln_g, *, n_heads, tq, nh, lam_init):
    b, s, dw = q.shape
    vd = dw // n_heads
    hd = vd // 2
    nq = s // tq
    assert n_heads % nh == 0
    return pl.pallas_call(
        functools.partial(_attn_prompt_kernel, tq=tq, hd=hd, nh=nh, lam_init=lam_init),
        grid=(b, n_heads // nh, nq),
        in_specs=[pl.BlockSpec((1, tq, nh * vd), lambda bi, h, qi: (bi, qi, h)),
                  pl.BlockSpec((1, s, nh * vd), lambda bi, h, qi: (bi, 0, h)),
                  pl.BlockSpec((1, s, nh * vd), lambda bi, h, qi: (bi, 0, h)),
                  pl.BlockSpec((nh, 4, tq, 2 * tq), lambda bi, h, qi: (h, 0, 0, 0)),
                  pl.BlockSpec((4, hd), lambda bi, h, qi: (0, 0)),
                  pl.BlockSpec((vd, 1), lambda bi, h, qi: (0, 0))],
        out_specs=pl.BlockSpec((1, tq, nh * vd), lambda bi, h, qi: (bi, qi, h)),
        out_shape=jax.ShapeDtypeStruct((b, s, dw), BF16),
        scratch_shapes=[pltpu.VMEM((nh, nq, vd + ONES_ROWS, tq), BF16), pltpu.VMEM((nh, tq, 2 * tq), F32),
                        pltpu.VMEM((nh, tq, 2 * tq), F32), pltpu.VMEM((nh, 1, 2 * tq), F32),
                        pltpu.VMEM((nh, vd + ONES_ROWS, 2 * tq), F32)],
        compiler_params=_cparams(("arbitrary", "arbitrary", "arbitrary")),
        name="attn_prompt",
    )(q, k, v, bias_p, lam_vecs, subln_g.reshape(vd, 1))


def _attn_decode_kernel(pt_ref, *refs, n_page_refs, n_heads, tq, hd, lam_init):
    k_refs = refs[:n_page_refs]
    v_refs = refs[n_page_refs:2 * n_page_refs]
    (q_ref, kn_ref, vn_ref, pen_ref, lam_ref, g_ref, o_ref, qz_sc, m_sc, l_sc, acc_sc) = refs[2 * n_page_refs:]
    del pt_ref
    step = pl.program_id(1)
    last = pl.num_programs(1) - 1
    vd = 2 * hd
    r2 = 2 * tq
    page = k_refs[0].shape[1] // n_heads
    nt_dims = (((1,), (1,)), ((), ()))

    @pl.when(step == 0)
    def _():
        lane = lax.broadcasted_iota(I32, (tq, vd), 1)
        for h in range(n_heads):
            qh = q_ref[:, h * vd:(h + 1) * vd]
            qz_sc[h] = jnp.concatenate([jnp.where(lane < hd, qh, 0.0), jnp.where(lane >= hd, qh, 0.0)],
                                       axis=0).astype(BF16)
        m_sc[...] = jnp.full(m_sc.shape, NEG_INF, F32)
        l_sc[...] = jnp.zeros(l_sc.shape, F32)
        acc_sc[...] = jnp.zeros(acc_sc.shape, F32)

    def update(head_blocks):
        all_s = []
        for h, blocks in enumerate(head_blocks):
            ss = []
            for kh, _, bias in blocks:
                s = lax.dot_general(qz_sc[h], kh.astype(BF16), nt_dims, preferred_element_type=F32)
                ss.append(s if bias is None else s + bias)
            all_s.append(ss)
        all_p, corrs = [], []
        for h, ss in enumerate(all_s):
            m = m_sc[h]
            m_new = jnp.maximum(m, jnp.max(functools.reduce(jnp.maximum, ss), axis=-1, keepdims=True))
            corr = jnp.exp2(m - m_new)
            ps = [jnp.exp2(s - m_new) for s in ss]
            l_sc[h] = l_sc[h] * corr + jnp.sum(functools.reduce(jnp.add, ps), axis=-1, keepdims=True)
            m_sc[h] = m_new
            all_p.append([p.astype(BF16) for p in ps])
            corrs.append(corr)
        for h, blocks in enumerate(head_blocks):
            acc = acc_sc[h] * corrs[h]
            for p, (_, vh, _) in zip(all_p[h], blocks):
                acc = acc + jnp.dot(p, vh.astype(BF16), preferred_element_type=F32)
            acc_sc[h] = acc

    is_last = (step == last).astype(F32)
    head_blocks = []
    for h in range(n_heads):
        rows = pl.ds(h, page, stride=n_heads)
        blocks = [(k_refs[i][0, rows, :], v_refs[i][0, rows, :], None) for i in range(n_page_refs - 1)]
        i = n_page_refs - 1
        blocks.append((k_refs[i][0, rows, :], v_refs[i][0, rows, :], pen_ref[0, h] * is_last))
        head_blocks.append(blocks)
    update(head_blocks)

    @pl.when(step == last)
    def _():
        pad = jnp.zeros((LANES - tq, vd), F32)
        lam = _diff_lambda_in_kernel(lam_ref, lam_init)
        head_blocks = []
        for h in range(n_heads):
            rows = pl.ds(h, tq, stride=n_heads)
            kh = jnp.concatenate([kn_ref[rows, :], pad], axis=0)
            vh = jnp.concatenate([vn_ref[rows, :], pad], axis=0)
            head_blocks.append([(kh, vh, pen_ref[1, h])])
        update(head_blocks)
        for h in range(n_heads):
            acc = acc_sc[h]
            l = l_sc[h]
            o = acc[0:tq] / l[0:tq] - lam * (acc[tq:r2] / l[tq:r2])
            ms = jnp.mean(o * o, axis=-1, keepdims=True)
            o_ref[:, h * vd:(h + 1) * vd] = (o * lax.rsqrt(ms + LN_EPS) * g_ref[...] * (1.0 - lam_init)
                                             ).astype(o_ref.dtype)


def _attn_decode(q, k_new, v_new, cache_k2, cache_v2, page_table, pen, lam_vecs, subln_g,
                 *, n_heads, tq, pages_per_step, lam_init):
    t, dw = q.shape
    bd, n_pages = page_table.shape
    _, page_rows, vd = cache_k2.shape
    hd = vd // 2
    pps = pages_per_step
    n_new = tq * n_heads
    assert n_pages % pps == 0 and tq % 8 == 0 and tq <= LANES <= page_rows // n_heads
    n_steps = n_pages // pps

    def page_spec(i):
        return pl.BlockSpec((1, page_rows, vd), lambda b, s, pt: (pt[b * n_pages + s * pps + i], 0, 0))

    tokb = pl.BlockSpec((tq, dw), lambda b, s, pt: (b, 0))
    newb = pl.BlockSpec((n_new, vd), lambda b, s, pt: (b, 0))
    full = lambda shape: pl.BlockSpec(shape, lambda b, s, pt: (0,) * len(shape))
    grid_spec = pltpu.PrefetchScalarGridSpec(
        num_scalar_prefetch=1,
        grid=(bd, n_steps),
        in_specs=[page_spec(i) for i in range(pps)] + [page_spec(i) for i in range(pps)]
        + [tokb, newb, newb, full(pen.shape), full((4, hd)), full((1, vd))],
        out_specs=tokb,
        scratch_shapes=[pltpu.VMEM((n_heads, 2 * tq, vd), BF16), pltpu.VMEM((n_heads, 2 * tq, 1), F32),
                        pltpu.VMEM((n_heads, 2 * tq, 1), F32), pltpu.VMEM((n_heads, 2 * tq, vd), F32)],
    )
    return pl.pallas_call(
        functools.partial(_attn_decode_kernel, n_page_refs=pps, n_heads=n_heads, tq=tq, hd=hd, lam_init=lam_init),
        grid_spec=grid_spec,
        out_shape=jax.ShapeDtypeStruct((t, dw), F32),
        compiler_params=_cparams(("arbitrary", "arbitrary")),
        name="attn_decode",
    )(page_table.reshape(-1), *([cache_k2] * pps), *([cache_v2] * pps), q, k_new, v_new, pen, lam_vecs,
      subln_g.reshape(1, vd))


def _outproj_kernel(x_ref, att_ref, sgu_ref, lng_ref, lnb_ref, wo_ref, l1g_ref, l1b_ref, wr_ref, br_ref,
                    h1_ref, eidx_ref, topw_ref, cnt_ref, *, alpha, dw, n_experts, n_groups, n_sub):
    i = pl.program_id(0)
    tm = x_ref.shape[0]
    ts = tm // n_sub
    epg = n_experts // n_groups
    lane = lax.broadcasted_iota(I32, (ts, LANES), 1).astype(F32)
    big = float(LANES)
    ninf = -jnp.inf
    is_g = (lane >= n_experts) & (lane < n_experts + n_groups)
    subs = [slice(c * ts, (c + 1) * ts) for c in range(n_sub)]
    mixes = [jnp.dot(att_ref[r, :].astype(BF16), wo_ref[0:dw, :], preferred_element_type=F32)
             + jnp.dot(sgu_ref[r, :].astype(BF16), wo_ref[dw:, :], preferred_element_type=F32) for r in subs]
    h1s = []
    for r, mix in zip(subs, mixes):
        hp = _ln(x_ref[r, :], lng_ref[...], lnb_ref[...])
        h1 = _ln(alpha * hp + mix, l1g_ref[...], l1b_ref[...])
        h1_ref[r, :] = h1
        h1s.append(h1.astype(BF16))
    all_logits = [jnp.dot(h1b, wr_ref[...], preferred_element_type=F32) + br_ref[...] for h1b in h1s]
    cnt = jnp.zeros((1, LANES), F32)
    for r, logits in zip(subs, all_logits):
        gmax = jnp.max(jnp.where(is_g, logits, ninf), axis=-1, keepdims=True)
        gidx = jnp.min(jnp.where(is_g & (logits == gmax), lane, big), axis=-1, keepdims=True) - n_experts
        gsum = jnp.sum(jnp.where(is_g, jnp.exp(logits - gmax), 0.0), axis=-1, keepdims=True)
        g_w = 1.0 / gsum
        sel = (lane >= gidx * epg) & (lane < (gidx + 1.0) * epg)
        v1 = jnp.max(jnp.where(sel, logits, ninf), axis=-1, keepdims=True)
        i1 = jnp.min(jnp.where(sel & (logits == v1), lane, big), axis=-1, keepdims=True)
        sel2 = sel & (lane != i1)
        v2 = jnp.max(jnp.where(sel2, logits, ninf), axis=-1, keepdims=True)
        i2 = jnp.min(jnp.where(sel2 & (logits == v2), lane, big), axis=-1, keepdims=True)
        e = jnp.exp(v2 - v1)
        eidx_ref[r, 0:1] = i1.astype(I32)
        eidx_ref[r, 1:2] = i2.astype(I32)
        topw_ref[r, 0:1] = g_w / (1.0 + e)
        topw_ref[r, 1:2] = g_w * e / (1.0 + e)
        onehot = ((lane == i1) | (lane == i2)).astype(F32)
        cnt = cnt + jnp.sum(onehot, axis=0, keepdims=True)

    @pl.when(i == 0)
    def _():
        cnt_ref[...] = jnp.zeros(cnt_ref.shape, F32)

    cnt_ref[...] += jnp.broadcast_to(cnt, cnt_ref.shape)


def _outproj_router(x2d, att, sgu, ln_g, ln_b, w_o_b, l1g, l1b, w_r_b, b_r, *, tm, alpha, n_experts, n_groups):
    t, d = x2d.shape
    n_sub = 2 if t % (2 * tm) == 0 else 1
    tm = n_sub * tm
    dw = att.shape[1]
    sw = sgu.shape[1]
    tok = lambda w: pl.BlockSpec((tm, w), lambda i: (i, 0))
    return pl.pallas_call(
        functools.partial(_outproj_kernel, alpha=alpha, dw=dw, n_experts=n_experts, n_groups=n_groups,
                          n_sub=n_sub),
        grid=(t // tm,),
        in_specs=[tok(d), tok(dw), tok(sw), _resident((1, d)), _resident((1, d)), _resident((dw + sw, d)),
                  _resident((1, d)), _resident((1, d)), _resident((d, LANES)), _resident((1, LANES))],
        out_specs=[tok(d), tok(TOP_K), tok(TOP_K), pl.BlockSpec((8, LANES), lambda i: (0, 0))],
        out_shape=[jax.ShapeDtypeStruct((t, d), F32), jax.ShapeDtypeStruct((t, TOP_K), I32),
                   jax.ShapeDtypeStruct((t, TOP_K), F32), jax.ShapeDtypeStruct((8, LANES), F32)],
        compiler_params=_cparams(("arbitrary",)),
        name="outproj_router",
    )(x2d, att, sgu, ln_g.reshape(1, d), ln_b.reshape(1, d), w_o_b, l1g.reshape(1, d), l1b.reshape(1, d),
      w_r_b, b_r)


def _pos_kernel(eidx_ref, base_ref, pos_ref, run_sc):
    i = pl.program_id(0)

    @pl.when(i == 0)
    def _():
        run_sc[...] = jnp.zeros(run_sc.shape, F32)

    tm = eidx_ref.shape[0]
    lane = lax.broadcasted_iota(I32, (tm, LANES), 1)
    oh0 = (lane == eidx_ref[:, 0:1]).astype(F32)
    oh1 = (lane == eidx_ref[:, 1:2]).astype(F32)
    oh = oh0 + oh1
    row = lax.broadcasted_iota(I32, (tm, tm), 0)
    col = lax.broadcasted_iota(I32, (tm, tm), 1)
    lower = (col < row).astype(BF16)
    rank = jnp.dot(lower, oh.astype(BF16), preferred_element_type=F32)
    posmat = rank + base_ref[...] + run_sc[...]
    for k, ohk in enumerate((oh0, oh1)):
        pk = jnp.sum(ohk * posmat, axis=-1, keepdims=True)
        pk_rows = jnp.broadcast_to(pk, (tm, LANES)).T
        pos_ref[0, k:k + 1, :] = pk_rows[0:1, :].astype(I32)
    run_sc[...] += jnp.sum(oh, axis=0, keepdims=True)


def _positions(eidx, base, *, tm):
    t = eidx.shape[0]
    nt = t // tm
    return pl.pallas_call(
        _pos_kernel,
        grid=(nt,),
        in_specs=[pl.BlockSpec((tm, TOP_K), lambda i: (i, 0)), pl.BlockSpec((1, LANES), lambda i: (0, 0))],
        out_specs=pl.BlockSpec((1, TOP_K, tm), lambda i: (i, 0, 0)),
        out_shape=jax.ShapeDtypeStruct((nt, TOP_K, tm), I32),
        scratch_shapes=[pltpu.VMEM((1, LANES), F32)],
        compiler_params=_cparams(("arbitrary",)),
        name="route_positions",
    )(eidx, base)


def _scatter_kernel(pos_ref, *refs, tm, tile_starts):
    src_refs, xs_hbm, sem = refs[:-2], refs[-2], refs[-1]
    i = pl.program_id(0)

    def run(src_ref):
        def row_copy(t, k):
            return pltpu.make_async_copy(src_ref.at[pl.ds(t, 1)], xs_hbm.at[pl.ds(pos_ref[0, k, t], 1)], sem)

        def issue(t, c):
            for k in range(TOP_K):
                row_copy(t, k).start()
            return c

        def drain(t, c):
            for k in range(TOP_K):
                row_copy(t, k).wait()
            return c

        lax.fori_loop(0, tm, issue, 0, unroll=8)
        lax.fori_loop(0, tm, drain, 0, unroll=8)

    for g, src_ref in enumerate(src_refs):
        pl.when((i >= tile_starts[g]) & (i < tile_starts[g + 1]))(functools.partial(run, src_ref))


def _scatter_rows(h1s, pos, *, tm):
    d = h1s[0].shape[1]
    tiles = [h.shape[0] // tm for h in h1s]
    tile_starts = [sum(tiles[:g]) for g in range(len(tiles) + 1)]
    nt = tile_starts[-1]

    def src_spec(g):
        return pl.BlockSpec((tm, d), lambda i: (jnp.clip(i - tile_starts[g], 0, tiles[g] - 1), 0))

    return pl.pallas_call(
        functools.partial(_scatter_kernel, tm=tm, tile_starts=tuple(tile_starts)),
        grid=(nt,),
        in_specs=[pl.BlockSpec((1, TOP_K, tm), lambda i: (i, 0, 0), memory_space=pltpu.SMEM)]
        + [src_spec(g) for g in range(len(h1s))],
        out_specs=pl.BlockSpec(memory_space=pl.ANY),
        out_shape=jax.ShapeDtypeStruct((TOP_K * nt * tm, d), h1s[0].dtype),
        scratch_shapes=[pltpu.SemaphoreType.DMA(())],
        compiler_params=_cparams(("arbitrary",)),
        name="scatter_to_experts",
    )(pos, *h1s)


def _expert_kernel(tile_ref, grp_ref, lo_ref, hi_ref, first_ref, x_ref, wg_ref, wu_ref, wd_ref, o_ref,
                   wgb, wub, wdb, *, tm):
    w = pl.program_id(0)
    changed = (w == 0) | (grp_ref[w] != grp_ref[jnp.maximum(w - 1, 0)])

    @pl.when(changed)
    def _():
        wgb[...] = wg_ref[0].astype(BF16)
        wub[...] = wu_ref[0].astype(BF16)
        wdb[...] = wd_ref[0].astype(BF16)

    xb = x_ref[...].astype(BF16)
    g = jnp.dot(xb, wgb[...], preferred_element_type=F32)
    u = jnp.dot(xb, wub[...], preferred_element_type=F32)
    hid = (g * _sigmoid(g) * u).astype(BF16)
    o = jnp.dot(hid, wdb[...], preferred_element_type=F32)
    row = tile_ref[w] * tm + lax.broadcasted_iota(I32, (tm, 1), 0)
    o = jnp.where((row >= lo_ref[w]) & (row < hi_ref[w]), o, 0.0)

    @pl.when(first_ref[w] == 1)
    def _():
        o_ref[...] = o

    @pl.when(first_ref[w] == 0)
    def _():
        o_ref[...] += o


def _expert_ffn(xs, w_gate, w_up, w_down, meta, *, tm):
    a, d = xs.shape
    n_experts, _, de = w_gate.shape
    tile_ids, grp_ids, lo, hi, first = meta
    n_work = tile_ids.shape[0]
    grid_spec = pltpu.PrefetchScalarGridSpec(
        num_scalar_prefetch=5,
        grid=(n_work,),
        in_specs=[pl.BlockSpec((tm, d), lambda w, ti, gi, lo_, hi_, fi: (ti[w], 0)),
                  pl.BlockSpec((1, d, de), lambda w, ti, gi, lo_, hi_, fi: (gi[w], 0, 0)),
                  pl.BlockSpec((1, d, de), lambda w, ti, gi, lo_, hi_, fi: (gi[w], 0, 0)),
                  pl.BlockSpec((1, de, d), lambda w, ti, gi, lo_, hi_, fi: (gi[w], 0, 0))],
        out_specs=pl.BlockSpec((tm, d), lambda w, ti, gi, lo_, hi_, fi: (ti[w], 0)),
        scratch_shapes=[pltpu.VMEM((d, de), BF16), pltpu.VMEM((d, de), BF16), pltpu.VMEM((de, d), BF16)],
    )
    return pl.pallas_call(
        functools.partial(_expert_kernel, tm=tm),
        grid_spec=grid_spec,
        out_shape=jax.ShapeDtypeStruct((a, d), F32),
        compiler_params=_cparams(("arbitrary",)),
        name="expert_ffn",
    )(tile_ids, grp_ids, lo, hi, first, xs, w_gate, w_up, w_down)


def _group_metadata(counts, *, n_rows, tm):
    n_experts = counts.shape[0]
    nt = n_rows // tm
    n_work = nt + n_experts - 1
    ends = jnp.cumsum(counts)
    starts = ends - counts
    ntiles_g = jnp.where(counts > 0, (ends - 1) // tm - starts // tm + 1, 0)
    work_end = jnp.cumsum(ntiles_g)
    work_start = work_end - ntiles_g
    total = work_end[-1]
    w = jnp.arange(n_work, dtype=I32)
    wc = jnp.minimum(w, total - 1)
    g = jnp.minimum(jnp.sum((work_end[None, :] <= wc[:, None]).astype(I32), axis=1), n_experts - 1)
    onehot = (g[:, None] == jnp.arange(n_experts, dtype=I32)[None, :]).astype(I32)
    pick = lambda a: jnp.sum(onehot * a[None, :], axis=1)
    starts_g, ends_g = pick(starts), pick(ends)
    tile = (starts_g // tm + (wc - pick(work_start))).astype(I32)
    valid = w < total
    lo = jnp.where(valid, jnp.maximum(starts_g, tile * tm), 0).astype(I32)
    hi = jnp.where(valid, jnp.minimum(ends_g, (tile + 1) * tm), 0).astype(I32)
    prev_tile = jnp.concatenate([jnp.full((1,), -1, I32), tile[:-1]])
    first = (tile != prev_tile).astype(I32)
    return tile, g, lo, hi, first


def _final_kernel(pos_ref, posn_ref, h1_ref, topw_ref, p_ref, os_hbm, l2g_ref, l2b_ref, wpg_ref, bpg_ref,
                  wpp_ref, y_ref, buf, sem, *, tm, alpha):
    i = pl.program_id(0)
    n = pl.num_programs(0)
    slot = i % 2

    def row_copy(pref, sl, t, k):
        return pltpu.make_async_copy(os_hbm.at[pl.ds(pref[0, k, t], 1)], buf.at[sl, k, pl.ds(t, 1)], sem.at[sl])

    def issue(pref, sl):
        def body(t, c):
            for k in range(TOP_K):
                row_copy(pref, sl, t, k).start()
            return c
        lax.fori_loop(0, tm, body, 0, unroll=8)

    @pl.when(i == 0)
    def _():
        issue(pos_ref, 0)

    @pl.when(i + 1 < n)
    def _():
        issue(posn_ref, 1 - slot)

    def drain(t, c):
        for k in range(TOP_K):
            row_copy(pos_ref, slot, t, k).wait()
        return c

    lax.fori_loop(0, tm, drain, 0, unroll=8)
    emb = jnp.dot(p_ref[...].astype(BF16), wpp_ref[...], preferred_element_type=F32)
    y = topw_ref[:, 0:1] * buf[slot, 0] + topw_ref[:, 1:2] * buf[slot, 1]
    h2 = _ln(alpha * h1_ref[...] + y, l2g_ref[...], l2b_ref[...])
    gate = _sigmoid(jnp.dot(h2.astype(BF16), wpg_ref[...], preferred_element_type=F32) + bpg_ref[...])
    y_ref[...] = h2 + gate * emb


def _combine_ple(h1, topw, pos, p2d, out_sorted, l2g, l2b, w_pg_b, b_pg, w_pp_b, *, tm, alpha):
    t, d = h1.shape
    nt = t // tm
    ple = p2d.shape[1]
    tok = lambda w: pl.BlockSpec((tm, w), lambda i: (i, 0))
    return pl.pallas_call(
        functools.partial(_final_kernel, tm=tm, alpha=alpha),
        grid=(nt,),
        in_specs=[pl.BlockSpec((1, TOP_K, tm), lambda i: (i, 0, 0), memory_space=pltpu.SMEM),
                  pl.BlockSpec((1, TOP_K, tm), lambda i: (jnp.minimum(i + 1, nt - 1), 0, 0),
                               memory_space=pltpu.SMEM),
                  tok(d), tok(TOP_K), tok(ple), pl.BlockSpec(memory_space=pl.ANY),
                  _resident((1, d)), _resident((1, d)), _resident((d, d)), _resident((1, d)),
                  _resident((ple, d))],
        out_specs=tok(d),
        out_shape=jax.ShapeDtypeStruct((t, d), F32),
        scratch_shapes=[pltpu.VMEM((2, TOP_K, tm, d), F32), pltpu.SemaphoreType.DMA((2,))],
        compiler_params=_cparams(("arbitrary",)),
        name="combine_ple",
    )(pos, pos, h1, topw, p2d, out_sorted, l2g.reshape(1, d), l2b.reshape(1, d), w_pg_b, b_pg.reshape(1, d),
      w_pp_b)


def _post_mix(groups, prm, *, tm, alpha, n_experts, n_groups):
    routed = [_outproj_router(x2d, att, sgu, prm["ln_emb_g"], prm["ln_emb_b"], prm["w_o"], prm["ln1_g"],
                              prm["ln1_b"], prm["w_r"], prm["b_r"], tm=tm, alpha=alpha, n_experts=n_experts,
                              n_groups=n_groups) for x2d, att, sgu, _ in groups]
    t_all = sum(g[0].shape[0] for g in groups)
    counts = sum(cnt[0, :n_experts] for _, _, _, cnt in routed).astype(I32)
    starts = jnp.cumsum(counts) - counts
    base = jnp.pad(starts.astype(F32), (0, LANES - n_experts)).reshape(1, LANES)
    eidx_all = routed[0][1] if len(routed) == 1 else jnp.concatenate([r[1] for r in routed], axis=0)
    pos = _positions(eidx_all, base, tm=tm)
    xs = _scatter_rows([r[0] for r in routed], pos, tm=tm)
    meta = _group_metadata(counts, n_rows=TOP_K * t_all, tm=tm)
    out_sorted = _expert_ffn(xs, prm["w_gate"], prm["w_up"], prm["w_down"], meta, tm=tm)
    outs, tile0 = [], 0
    for (x2d, _, _, p2d), (h1, _, topw, _) in zip(groups, routed):
        tiles = x2d.shape[0] // tm
        outs.append(_combine_ple(h1, topw, pos[tile0:tile0 + tiles], p2d, out_sorted, prm["ln2_g"], prm["ln2_b"],
                                 prm["w_pg"], prm["b_pg"], prm["w_pp"], tm=tm, alpha=alpha))
        tile0 += tiles
    return outs


def kernel(x_prompt, x_sample, cache_k, cache_v, page_table, p_prompt, p_sample, ln_emb_g, ln_emb_b, w_in, lambda_q1, lambda_k1, lambda_q2, lambda_k2, subln_g, rel_bias, sgu_ln_g, sgu_ln_b, sgu_w, sgu_b, w_o, ln1_g, ln1_b, w_router_group, b_router_group, w_router_expert, b_router_expert, w_gate, w_up, w_down, ln2_g, ln2_b, w_ple_gate, b_ple_gate, w_ple_proj):
    bsz, seq, d = x_prompt.shape
    bd, tq_dec, _ = x_sample.shape
    depth, _, page, n_heads, vd = cache_v.shape
    hd = vd // 2
    dw = n_heads * vd
    sw = (w_in.shape[-1] - 3 * dw) // 2
    n_groups = w_router_group.shape[-1]
    n_experts = w_router_expert.shape[-1]
    n_buckets = rel_bias.shape[0]
    ple = p_prompt.shape[-1]
    alpha = (2.0 * depth) ** 0.25
    assert n_experts + n_groups <= LANES

    tp, ts = bsz * seq, bd * tq_dec
    tm_p = min(256, tp)
    tm_s = min(256, ts)
    tq = min(256, seq)
    n_pages = page_table.shape[1]
    pps = math.gcd(n_pages, 8)

    d_const = _const_bucket_distance(n_buckets)
    assert tq + 1 >= d_const and page + 1 >= d_const
    qi = np.arange(tq)[:, None]
    kj = np.arange(tq)[None, :]
    bkt_p = np.concatenate([np.where(qi <= kj, _t5_bucket(kj - qi, n_buckets), -1).astype(np.int32),
                            _t5_bucket(tq + kj - qi, n_buckets), np.full((tq, tq), n_buckets - 1, np.int32),
                            np.full((tq, tq), -1, np.int32)], axis=0)
    bias_p = _bias_tables(rel_bias, np.tile(bkt_p, (1, 2))).reshape(n_heads, 4, tq, 2 * tq)
    r2 = 2 * tq_dec
    di = np.tile(np.arange(tq_dec), 2)[:, None]
    dj = np.arange(page)[None, :]
    bkt_last = _t5_bucket(page + di - dj, n_buckets)
    bkt_new = np.where((dj <= di) & (dj < tq_dec), _t5_bucket(di - dj, n_buckets), -1).astype(np.int32)
    pen = _bias_tables(rel_bias, np.concatenate([bkt_last, bkt_new], axis=0))
    pen = pen.reshape(n_heads, 2, r2, page).transpose(1, 0, 2, 3)

    xp = x_prompt.reshape(tp, d)
    xs = x_sample.reshape(ts, d)
    hp_x, hs_x = xp, xs
    k_p_rows, v_p_rows, k_s_rows, v_s_rows, sgu_s_rows = [], [], [], [], []
    for l in range(depth):
        assert depth == 1, "the trunk input of deeper layers is the previous layer's output"
        lam_init = 0.8 - 0.6 * math.exp(-0.3 * l)
        lam_vecs = jnp.stack([lambda_q1[l], lambda_k1[l], lambda_q2[l], lambda_k2[l]])
        w_in_b = w_in[l].astype(BF16)
        pad_r = LANES - n_experts - n_groups
        prm = dict(
            ln_emb_g=ln_emb_g, ln_emb_b=ln_emb_b, w_o=w_o[l].astype(BF16), ln1_g=ln1_g[l], ln1_b=ln1_b[l],
            w_r=jnp.pad(jnp.concatenate([w_router_expert[l], w_router_group[l]], axis=1),
                        ((0, 0), (0, pad_r))).astype(BF16),
            b_r=jnp.pad(jnp.concatenate([b_router_expert[l], b_router_group[l]]), (0, pad_r)).reshape(1, LANES),
            w_gate=w_gate[l], w_up=w_up[l], w_down=w_down[l], ln2_g=ln2_g[l], ln2_b=ln2_b[l],
            w_pg=w_ple_gate[l].astype(BF16), b_pg=b_ple_gate[l], w_pp=w_ple_proj[l].astype(BF16))
        common = dict(dw=dw, sw=sw, n_heads=n_heads, q_scale=hd ** -0.5 * LOG2E)
        qp, kp, vp, kbp, vbp, sgu_p = _inproj(
            hp_x, ln_emb_g, ln_emb_b, w_in_b, sgu_ln_g[l], sgu_ln_b[l], sgu_w[l], sgu_b[l],
            seq=seq, tm=tm_p, q_dtype=BF16, emit_vnorm=False, **common)
        qs, ks, vs, _, _, sgu_s, vn_s = _inproj(
            hs_x, ln_emb_g, ln_emb_b, w_in_b, sgu_ln_g[l], sgu_ln_b[l], sgu_w[l], sgu_b[l],
            seq=tq_dec, tm=tm_s, q_dtype=F32, emit_vnorm=True, **common)
        att_p = _attn_prompt(qp.reshape(bsz, seq, dw), kbp.reshape(bsz, seq, dw), vbp.reshape(bsz, seq, dw),
                             bias_p, lam_vecs, subln_g[l], n_heads=n_heads, tq=tq, nh=math.gcd(n_heads, 4),
                             lam_init=lam_init)
        ck = (cache_k if depth == 1 else cache_k[l]).reshape(-1, page * n_heads, vd)
        cv = (cache_v if depth == 1 else cache_v[l]).reshape(-1, page * n_heads, vd)
        att_s = _attn_decode(qs, ks.reshape(-1, vd), vs.reshape(-1, vd), ck, cv, page_table, pen, lam_vecs,
                             subln_g[l], n_heads=n_heads, tq=tq_dec, pages_per_step=pps, lam_init=lam_init)
        post = dict(alpha=alpha, n_experts=n_experts, n_groups=n_groups)
        grp_p = (hp_x, att_p.reshape(tp, dw), sgu_p, p_prompt[l].reshape(tp, ple))
        grp_s = (hs_x, att_s, sgu_s, p_sample[l].reshape(ts, ple))
        if tm_p == tm_s:
            hp_x, hs_x = _post_mix([grp_p, grp_s], prm, tm=tm_p, **post)
        else:
            (hp_x,), (hs_x,) = _post_mix([grp_p], prm, tm=tm_p, **post), _post_mix([grp_s], prm, tm=tm_s, **post)
        k_p_rows.append(kp.reshape(bsz, seq, n_heads, vd))
        v_p_rows.append(vp.reshape(bsz, seq, n_heads, vd))
        k_s_rows.append(ks.reshape(bd, tq_dec, n_heads, vd))
        v_s_rows.append(vs.reshape(bd, tq_dec, n_heads, vd))
        sgu_s_rows.append(vn_s.reshape(bd, tq_dec, sw))
    stack = lambda rows: rows[0][None] if len(rows) == 1 else jnp.stack(rows)
    return (hp_x.reshape(bsz, seq, d), hs_x.reshape(bd, tq_dec, d), stack(k_p_rows), stack(v_p_rows),
            stack(k_s_rows), stack(v_s_rows), stack(sgu_s_rows))
```

```python
import functools
import math

import numpy as np
import jax
import jax.numpy as jnp
from jax import lax
from jax.experimental import pallas as pl
from jax.experimental.pallas import tpu as pltpu

F32 = jnp.float32
BF16 = jnp.bfloat16
I32 = jnp.int32

LN_EPS = 1e-5
NEG_INF = -1e30
MAX_DISTANCE = 128
TOP_K = 2
LANES = 128
ONES_ROWS = 16
LOG2E = math.log2(math.e)
VMEM_LIMIT = 56 * 1024 * 1024


def _cparams(sem, vmem=VMEM_LIMIT):
    return pltpu.CompilerParams(dimension_semantics=sem, vmem_limit_bytes=vmem)


def _ln(x, g, b):
    mu = jnp.mean(x, axis=-1, keepdims=True)
    xc = x - mu
    var = jnp.mean(xc * xc, axis=-1, keepdims=True)
    return xc * lax.rsqrt(var + LN_EPS) * g + b


def _gelu_tanh(x):
    c = math.sqrt(2.0 / math.pi)
    return 0.5 * x * (1.0 + jnp.tanh(c * (x + 0.044715 * (x * x * x))))


def _sigmoid(x):
    return 1.0 / (1.0 + jnp.exp(-x))


def _resident(shape):
    nd = len(shape)
    return pl.BlockSpec(shape, lambda *_: (0,) * nd, pipeline_mode=pl.Buffered(1))


def _t5_bucket(dist, n_buckets):
    def run(ft):
        n = np.maximum(dist, 0)
        max_exact = n_buckets // 2
        nf = np.maximum(n, max_exact).astype(ft)
        large = max_exact + (np.log(nf / ft(max_exact)) / ft(math.log(MAX_DISTANCE / max_exact))
                             * ft(n_buckets - max_exact)).astype(np.int32)
        large = np.minimum(large, n_buckets - 1)
        return np.where(n < max_exact, n, large).astype(np.int32)
    b32, b64 = run(np.float32), run(np.float64)
    assert (b32 == b64).all(), "bucket boundaries are precision sensitive"
    return b32


def _const_bucket_distance(n_buckets):
    d = np.arange(0, 4 * MAX_DISTANCE)
    b = _t5_bucket(d, n_buckets)
    below = np.nonzero(b != n_buckets - 1)[0]
    return int(below.max()) + 1


def _bias_kernel(rb_ref, bkt_ref, out_ref, *, n_buckets):
    h = pl.program_id(0)
    bkt = bkt_ref[...]
    c_last = rb_ref[n_buckets - 1, h]
    acc = jnp.zeros(bkt.shape, F32)
    for b in range(n_buckets - 1):
        acc = jnp.where(bkt == b, (rb_ref[b, h] - c_last) * LOG2E, acc)
    out_ref[0] = jnp.where(bkt < 0, NEG_INF, acc)


def _bias_tables(rel_bias, bucket_np):
    n_buckets, n_heads = rel_bias.shape
    r, c = bucket_np.shape
    return pl.pallas_call(
        functools.partial(_bias_kernel, n_buckets=n_buckets),
        grid=(n_heads,),
        in_specs=[pl.BlockSpec(memory_space=pltpu.SMEM),
                  pl.BlockSpec((r, c), lambda h: (0, 0))],
        out_specs=pl.BlockSpec((1, r, c), lambda h: (h, 0, 0)),
        out_shape=jax.ShapeDtypeStruct((n_heads, r, c), F32),
        compiler_params=_cparams(("arbitrary",)),
        name="bias_tables",
    )(rel_bias, jnp.asarray(bucket_np))


def _inproj_kernel(x_ref, lng_ref, lnb_ref, w_ref, sg_ref, sb_ref, wmix_ref, bmix_ref,
                   q_ref, k4_ref, v4_ref, kb_ref, vb_ref, s_ref, *vn_refs, chunk, dw, sw, n_groups, q_scale):
    xn = _ln(x_ref[...], lng_ref[...], lnb_ref[...])
    xb = xn.astype(BF16)

    def proj(lo, n):
        return jnp.dot(xb, w_ref[:, lo:lo + n], preferred_element_type=F32)

    zu = proj(3 * dw, sw)
    zg = proj(3 * dw + sw, sw)
    q_ref[...] = (proj(0, dw) * q_scale).astype(q_ref.dtype)
    n_heads, vd = k4_ref.shape[1:]
    for lo, r4, rb in ((dw, k4_ref, kb_ref), (2 * dw, v4_ref, vb_ref)):
        val = proj(lo, dw)
        rb[...] = val.astype(BF16)
        for h in range(n_heads):
            r4[:, h, :] = val[:, h * vd:(h + 1) * vd]
    u = _gelu_tanh(zu)
    vn = _ln(_gelu_tanh(zg), sg_ref[...], sb_ref[...])
    if vn_refs:
        vn_refs[0][...] = vn
    vb = vn.astype(BF16)
    tm = xb.shape[0]
    shift = chunk.bit_length() - 1
    row = lax.broadcasted_iota(I32, (tm, tm), 0)
    col = lax.broadcasted_iota(I32, (tm, tm), 1)
    causal = ((row >> shift) == (col >> shift)) & (col <= row)
    gw = sw // n_groups
    for g in range(n_groups):
        mg = jnp.where(causal, wmix_ref[g], 0.0).astype(BF16)
        sv = jnp.dot(mg, vb[:, g * gw:(g + 1) * gw], preferred_element_type=F32) + bmix_ref[:, g:g + 1]
        s_ref[:, g * gw:(g + 1) * gw] = (u[:, g * gw:(g + 1) * gw] * sv).astype(s_ref.dtype)


def _inproj(x2d, ln_g, ln_b, w_in_b, sgu_g, sgu_b_ln, sgu_w, sgu_bias, *, seq, tm, dw, sw, n_heads,
            q_scale, q_dtype, emit_vnorm):
    t, d = x2d.shape
    vd = dw // n_heads
    n_groups, chunk_full, _ = sgu_w.shape
    chunk = min(chunk_full, seq)
    assert chunk & (chunk - 1) == 0 and seq % chunk == 0 and tm % chunk == 0 and t % tm == 0
    rep = tm // chunk
    wmix = jnp.tile(sgu_w[:, :chunk, :chunk], (1, rep, rep))
    bmix = jnp.tile(sgu_bias[:, :chunk].T, (rep, 1))
    cols = w_in_b.shape[1]
    tok = lambda w: pl.BlockSpec((tm, w), lambda i: (i, 0))
    tok4 = pl.BlockSpec((tm, n_heads, vd), lambda i: (i, 0, 0))
    out_shape = [jax.ShapeDtypeStruct((t, dw), q_dtype), jax.ShapeDtypeStruct((t, n_heads, vd), F32),
                 jax.ShapeDtypeStruct((t, n_heads, vd), F32), jax.ShapeDtypeStruct((t, dw), BF16),
                 jax.ShapeDtypeStruct((t, dw), BF16), jax.ShapeDtypeStruct((t, sw), BF16)]
    out_specs = [tok(dw), tok4, tok4, tok(dw), tok(dw), tok(sw)]
    if emit_vnorm:
        out_shape.append(jax.ShapeDtypeStruct((t, sw), F32))
        out_specs.append(tok(sw))
    return pl.pallas_call(
        functools.partial(_inproj_kernel, chunk=chunk, dw=dw, sw=sw, n_groups=n_groups, q_scale=q_scale),
        grid=(t // tm,),
        in_specs=[tok(d), _resident((1, d)), _resident((1, d)), _resident((d, cols)),
                  _resident((1, sw)), _resident((1, sw)), _resident((n_groups, tm, tm)),
                  _resident((tm, n_groups))],
        out_specs=out_specs,
        out_shape=out_shape,
        compiler_params=_cparams(("arbitrary",)),
        name="inproj_sgu",
    )(x2d, ln_g.reshape(1, d), ln_b.reshape(1, d), w_in_b, sgu_g.reshape(1, sw), sgu_b_ln.reshape(1, sw),
      wmix, bmix)


def _diff_lambda_in_kernel(lam_ref, lam_init):
    lv = lam_ref[...]
    a = jnp.sum(lv[0:1] * lv[1:2], axis=-1, keepdims=True)
    b = jnp.sum(lv[2:3] * lv[3:4], axis=-1, keepdims=True)
    return jnp.exp(a) - jnp.exp(b) + lam_init


def _attn_prompt_kernel(q_ref, k_ref, v_ref, bias_ref, lam_ref, g_ref, o_ref, vt_sc, sa_sc, sb_sc, m_sc, acc_sc,
                        *, tq, hd, nh, lam_init):
    qi = pl.program_id(2)
    nk = vt_sc.shape[1]
    vd = 2 * hd
    heads = range(nh)
    hcols = lambda hh: slice(hh * vd, (hh + 1) * vd)

    @pl.when(qi == 0)
    def _():
        for hh in heads:
            for c in range(nk):
                vt_sc[hh, c, 0:vd, :] = v_ref[0, c * tq:(c + 1) * tq, hcols(hh)].astype(F32).T.astype(BF16)
                vt_sc[hh, c, vd:, :] = jnp.ones((ONES_ROWS, tq), BF16)

    sub = lax.broadcasted_iota(I32, (vd, tq), 0)
    qq = []
    for hh in heads:
        qt = q_ref[0, :, hcols(hh)].astype(F32).T
        qq.append(jnp.concatenate([jnp.where(sub < hd, qt, 0.0), jnp.where(sub >= hd, qt, 0.0)],
                                  axis=1).astype(BF16))

    def scores_to(s_sc, j):
        rows = pl.ds(pl.multiple_of(j * tq, tq), tq)
        for hh in heads:
            s_sc[hh] = jnp.dot(k_ref[0, rows, hcols(hh)], qq[hh], preferred_element_type=F32)

    def update_from(s_sc, j):
        d = qi - j
        kind = jnp.where(d < 0, 3, jnp.minimum(d, 2))
        j = jnp.minimum(j, nk - 1)
        for hh in heads:
            s = s_sc[hh] + bias_ref[hh, kind]
            m = m_sc[hh]
            m_new = jnp.maximum(m, jnp.max(s, axis=0, keepdims=True))
            p = jnp.exp2((s - m_new).astype(BF16))
            acc_sc[hh] = acc_sc[hh] * jnp.exp2(m - m_new) + jnp.dot(vt_sc[hh, j], p, preferred_element_type=F32)
            m_sc[hh] = m_new

    m_sc[...] = jnp.full(m_sc.shape, NEG_INF, F32)
    acc_sc[...] = jnp.zeros(acc_sc.shape, F32)
    scores_to(sa_sc, 0)

    def pair(i, c):
        j = 2 * i
        scores_to(sb_sc, jnp.minimum(j + 1, nk - 1))
        update_from(sa_sc, j)
        scores_to(sa_sc, jnp.minimum(j + 2, nk - 1))
        update_from(sb_sc, j + 1)
        return c

    lax.fori_loop(0, (qi + 2) // 2, pair, 0)
    lam = _diff_lambda_in_kernel(lam_ref, lam_init)
    for hh in heads:
        acc = acc_sc[hh]
        l = acc[vd:vd + 1, :]
        ot = acc[0:vd, :tq] / l[:, :tq] - lam * (acc[0:vd, tq:] / l[:, tq:])
        ms = jnp.mean(ot * ot, axis=0, keepdims=True)
        ot = ot * lax.rsqrt(ms + LN_EPS) * g_ref[...] * (1.0 - lam_init)
        o_ref[0, :, hcols(hh)] = ot.T.astype(o_ref.dtype)


def _attn_prompt(q, k, v, bias_p, lam_vecs, subln_g, *, n_heads, tq, nh, lam_init):
    b, s, dw = q.shape
    vd = dw // n_heads
    hd = vd // 2
    nq = s // tq
    assert n_heads % nh == 0
    return pl.pallas_call(
        functools.partial(_attn_prompt_kernel, tq=tq, hd=hd, nh=nh, lam_init=lam_init),
        grid=(b, n_heads // nh, nq),
        in_specs=[pl.BlockSpec((1, tq, nh * vd), lambda bi, h, qi: (bi, qi, h)),
                  pl.BlockSpec((1, s, nh * vd), lambda bi, h, qi: (bi, 0, h)),
                  pl.BlockSpec((1, s, nh * vd), lambda bi, h, qi: (bi, 0, h)),
                  pl.BlockSpec((nh, 4, tq, 2 * tq), lambda bi, h, qi: (h, 0, 0, 0)),
                  pl.BlockSpec((4, hd), lambda bi, h, qi: (0, 0)),
                  pl.BlockSpec((vd, 1), lambda bi, h, qi: (0, 0))],
        out_specs=pl.BlockSpec((1, tq, nh * vd), lambda bi, h, qi: (bi, qi, h)),
        out_shape=jax.ShapeDtypeStruct((b, s, dw), BF16),
        scratch_shapes=[pltpu.VMEM((nh, nq, vd + ONES_ROWS, tq), BF16), pltpu.VMEM((nh, tq, 2 * tq), F32),
                        pltpu.VMEM((nh, tq, 2 * tq), F32), pltpu.VMEM((nh, 1, 2 * tq), F32),
                        pltpu.VMEM((nh, vd + ONES_ROWS, 2 * tq), F32)],
        compiler_params=_cparams(("arbitrary", "arbitrary", "arbitrary")),
        name="attn_prompt",
    )(q, k, v, bias_p, lam_vecs, subln_g.reshape(vd, 1))


def _attn_decode_kernel(pt_ref, *refs, n_page_refs, n_heads, tq, hd, lam_init):
    k_refs = refs[:n_page_refs]
    v_refs = refs[n_page_refs:2 * n_page_refs]
    (q_ref, kn_ref, vn_ref, pen_ref, lam_ref, g_ref, o_ref, qz_sc, m_sc, l_sc, acc_sc) = refs[2 * n_page_refs:]
    del pt_ref
    step = pl.program_id(1)
    last = pl.num_programs(1) - 1
    vd = 2 * hd
    r2 = 2 * tq
    page = k_refs[0].shape[1] // n_heads
    nt_dims = (((1,), (1,)), ((), ()))

    @pl.when(step == 0)
    def _():
        lane = lax.broadcasted_iota(I32, (tq, vd), 1)
        for h in range(n_heads):
            qh = q_ref[:, h * vd:(h + 1) * vd]
            qz_sc[h] = jnp.concatenate([jnp.where(lane < hd, qh, 0.0), jnp.where(lane >= hd, qh, 0.0)],
                                       axis=0).astype(BF16)
        m_sc[...] = jnp.full(m_sc.shape, NEG_INF, F32)
        l_sc[...] = jnp.zeros(l_sc.shape, F32)
        acc_sc[...] = jnp.zeros(acc_sc.shape, F32)

    def update(head_blocks):
        all_s = []
        for h, blocks in enumerate(head_blocks):
            ss = []
            for kh, _, bias in blocks:
                s = lax.dot_general(qz_sc[h], kh.astype(BF16), nt_dims, preferred_element_type=F32)
                ss.append(s if bias is None else s + bias)
            all_s.append(ss)
        all_p, corrs = [], []
        for h, ss in enumerate(all_s):
            m = m_sc[h]
            m_new = jnp.maximum(m, jnp.max(functools.reduce(jnp.maximum, ss), axis=-1, keepdims=True))
            corr = jnp.exp2(m - m_new)
            ps = [jnp.exp2(s - m_new) for s in ss]
            l_sc[h] = l_sc[h] * corr + jnp.sum(functools.reduce(jnp.add, ps), axis=-1, keepdims=True)
            m_sc[h] = m_new
            all_p.append([p.astype(BF16) for p in ps])
            corrs.append(corr)
        for h, blocks in enumerate(head_blocks):
            acc = acc_sc[h] * corrs[h]
            for p, (_, vh, _) in zip(all_p[h], blocks):
                acc = acc + jnp.dot(p, vh.astype(BF16), preferred_element_type=F32)
            acc_sc[h] = acc

    is_last = (step == last).astype(F32)
    head_blocks = []
    for h in range(n_heads):
        rows = pl.ds(h, page, stride=n_heads)
        blocks = [(k_refs[i][0, rows, :], v_refs[i][0, rows, :], None) for i in range(n_page_refs - 1)]
        i = n_page_refs - 1
        blocks.append((k_refs[i][0, rows, :], v_refs[i][0, rows, :], pen_ref[0, h] * is_last))
        head_blocks.append(blocks)
    update(head_blocks)

    @pl.when(step == last)
    def _():
        pad = jnp.zeros((LANES - tq, vd), F32)
        lam = _diff_lambda_in_kernel(lam_ref, lam_init)
        head_blocks = []
        for h in range(n_heads):
            rows = pl.ds(h, tq, stride=n_heads)
            kh = jnp.concatenate([kn_ref[rows, :], pad], axis=0)
            vh = jnp.concatenate([vn_ref[rows, :], pad], axis=0)
            head_blocks.append([(kh, vh, pen_ref[1, h])])
        update(head_blocks)
        for h in range(n_heads):
            acc = acc_sc[h]
            l = l_sc[h]
            o = acc[0:tq] / l[0:tq] - lam * (acc[tq:r2] / l[tq:r2])
            ms = jnp.mean(o * o, axis=-1, keepdims=True)
            o_ref[:, h * vd:(h + 1) * vd] = (o * lax.rsqrt(ms + LN_EPS) * g_ref[...] * (1.0 - lam_init)
                                             ).astype(o_ref.dtype)


def _attn_decode(q, k_new, v_new, cache_k2, cache_v2, page_table, pen, lam_vecs, subln_g,
                 *, n_heads, tq, pages_per_step, lam_init):
    t, dw = q.shape
    bd, n_pages = page_table.shape
    _, page_rows, vd = cache_k2.shape
    hd = vd // 2
    pps = pages_per_step
    n_new = tq * n_heads
    assert n_pages % pps == 0 and tq % 8 == 0 and tq <= LANES <= page_rows // n_heads
    n_steps = n_pages // pps

    def page_spec(i):
        return pl.BlockSpec((1, page_rows, vd), lambda b, s, pt: (pt[b * n_pages + s * pps + i], 0, 0))

    tokb = pl.BlockSpec((tq, dw), lambda b, s, pt: (b, 0))
    newb = pl.BlockSpec((n_new, vd), lambda b, s, pt: (b, 0))
    full = lambda shape: pl.BlockSpec(shape, lambda b, s, pt: (0,) * len(shape))
    grid_spec = pltpu.PrefetchScalarGridSpec(
        num_scalar_prefetch=1,
        grid=(bd, n_steps),
        in_specs=[page_spec(i) for i in range(pps)] + [page_spec(i) for i in range(pps)]
        + [tokb, newb, newb, full(pen.shape), full((4, hd)), full((1, vd))],
        out_specs=tokb,
        scratch_shapes=[pltpu.VMEM((n_heads, 2 * tq, vd), BF16), pltpu.VMEM((n_heads, 2 * tq, 1), F32),
                        pltpu.VMEM((n_heads, 2 * tq, 1), F32), pltpu.VMEM((n_heads, 2 * tq, vd), F32)],
    )
    return pl.pallas_call(
        functools.partial(_attn_decode_kernel, n_page_refs=pps, n_heads=n_heads, tq=tq, hd=hd, lam_init=lam_init),
        grid_spec=grid_spec,
        out_shape=jax.ShapeDtypeStruct((t, dw), F32),
        compiler_params=_cparams(("arbitrary", "arbitrary")),
        name="attn_decode",
    )(page_table.reshape(-1), *([cache_k2] * pps), *([cache_v2] * pps), q, k_new, v_new, pen, lam_vecs,
      subln_g.reshape(1, vd))


def _outproj_kernel(x_ref, att_ref, sgu_ref, lng_ref, lnb_ref, wo_ref, l1g_ref, l1b_ref, wr_ref, br_ref,
                    h1_ref, eidx_ref, topw_ref, cnt_ref, *, alpha, dw, n_experts, n_groups, n_sub):
    i = pl.program_id(0)
    tm = x_ref.shape[0]
    ts = tm // n_sub
    epg = n_experts // n_groups
    lane = lax.broadcasted_iota(I32, (ts, LANES), 1).astype(F32)
    big = float(LANES)
    ninf = -jnp.inf
    is_g = (lane >= n_experts) & (lane < n_experts + n_groups)
    subs = [slice(c * ts, (c + 1) * ts) for c in range(n_sub)]
    mixes = [jnp.dot(att_ref[r, :].astype(BF16), wo_ref[0:dw, :], preferred_element_type=F32)
             + jnp.dot(sgu_ref[r, :].astype(BF16), wo_ref[dw:, :], preferred_element_type=F32) for r in subs]
    h1s = []
    for r, mix in zip(subs, mixes):
        hp = _ln(x_ref[r, :], lng_ref[...], lnb_ref[...])
        h1 = _ln(alpha * hp + mix, l1g_ref[...], l1b_ref[...])
        h1_ref[r, :] = h1
        h1s.append(h1.astype(BF16))
    all_logits = [jnp.dot(h1b, wr_ref[...], preferred_element_type=F32) + br_ref[...] for h1b in h1s]
    cnt = jnp.zeros((1, LANES), F32)
    for r, logits in zip(subs, all_logits):
        gmax = jnp.max(jnp.where(is_g, logits, ninf), axis=-1, keepdims=True)
        gidx = jnp.min(jnp.where(is_g & (logits == gmax), lane, big), axis=-1, keepdims=True) - n_experts
        gsum = jnp.sum(jnp.where(is_g, jnp.exp(logits - gmax), 0.0), axis=-1, keepdims=True)
        g_w = 1.0 / gsum
        sel = (lane >= gidx * epg) & (lane < (gidx + 1.0) * epg)
        v1 = jnp.max(jnp.where(sel, logits, ninf), axis=-1, keepdims=True)
        i1 = jnp.min(jnp.where(sel & (logits == v1), lane, big), axis=-1, keepdims=True)
        sel2 = sel & (lane != i1)
        v2 = jnp.max(jnp.where(sel2, logits, ninf), axis=-1, keepdims=True)
        i2 = jnp.min(jnp.where(sel2 & (logits == v2), lane, big), axis=-1, keepdims=True)
        e = jnp.exp(v2 - v1)
        eidx_ref[r, 0:1] = i1.astype(I32)
        eidx_ref[r, 1:2] = i2.astype(I32)
        topw_ref[r, 0:1] = g_w / (1.0 + e)
        topw_ref[r, 1:2] = g_w * e / (1.0 + e)
        onehot = ((lane == i1) | (lane == i2)).astype(F32)
        cnt = cnt + jnp.sum(onehot, axis=0, keepdims=True)

    @pl.when(i == 0)
    def _():
        cnt_ref[...] = jnp.zeros(cnt_ref.shape, F32)

    cnt_ref[...] += jnp.broadcast_to(cnt, cnt_ref.shape)


def _outproj_router(x2d, att, sgu, ln_g, ln_b, w_o_b, l1g, l1b, w_r_b, b_r, *, tm, alpha, n_experts, n_groups):
    t, d = x2d.shape
    n_sub = 2 if t % (2 * tm) == 0 else 1
    tm = n_sub * tm
    dw = att.shape[1]
    sw = sgu.shape[1]
    tok = lambda w: pl.BlockSpec((tm, w), lambda i: (i, 0))
    return pl.pallas_call(
        functools.partial(_outproj_kernel, alpha=alpha, dw=dw, n_experts=n_experts, n_groups=n_groups,
                          n_sub=n_sub),
        grid=(t // tm,),
        in_specs=[tok(d), tok(dw), tok(sw), _resident((1, d)), _resident((1, d)), _resident((dw + sw, d)),
                  _resident((1, d)), _resident((1, d)), _resident((d, LANES)), _resident((1, LANES))],
        out_specs=[tok(d), tok(TOP_K), tok(TOP_K), pl.BlockSpec((8, LANES), lambda i: (0, 0))],
        out_shape=[jax.ShapeDtypeStruct((t, d), F32), jax.ShapeDtypeStruct((t, TOP_K), I32),
                   jax.ShapeDtypeStruct((t, TOP_K), F32), jax.ShapeDtypeStruct((8, LANES), F32)],
        compiler_params=_cparams(("arbitrary",)),
        name="outproj_router",
    )(x2d, att, sgu, ln_g.reshape(1, d), ln_b.reshape(1, d), w_o_b, l1g.reshape(1, d), l1b.reshape(1, d),
      w_r_b, b_r)


def _pos_kernel(eidx_ref, base_ref, pos_ref, run_sc):
    i = pl.program_id(0)

    @pl.when(i == 0)
    def _():
        run_sc[...] = jnp.zeros(run_sc.shape, F32)

    tm = eidx_ref.shape[0]
    lane = lax.broadcasted_iota(I32, (tm, LANES), 1)
    oh0 = (lane == eidx_ref[:, 0:1]).astype(F32)
    oh1 = (lane == eidx_ref[:, 1:2]).astype(F32)
    oh = oh0 + oh1
    row = lax.broadcasted_iota(I32, (tm, tm), 0)
    col = lax.broadcasted_iota(I32, (tm, tm), 1)
    lower = (col < row).astype(BF16)
    rank = jnp.dot(lower, oh.astype(BF16), preferred_element_type=F32)
    posmat = rank + base_ref[...] + run_sc[...]
    for k, ohk in enumerate((oh0, oh1)):
        pk = jnp.sum(ohk * posmat, axis=-1, keepdims=True)
        pk_rows = jnp.broadcast_to(pk, (tm, LANES)).T
        pos_ref[0, k:k + 1, :] = pk_rows[0:1, :].astype(I32)
    run_sc[...] += jnp.sum(oh, axis=0, keepdims=True)


def _positions(eidx, base, *, tm):
    t = eidx.shape[0]
    nt = t // tm
    return pl.pallas_call(
        _pos_kernel,
        grid=(nt,),
        in_specs=[pl.BlockSpec((tm, TOP_K), lambda i: (i, 0)), pl.BlockSpec((1, LANES), lambda i: (0, 0))],
        out_specs=pl.BlockSpec((1, TOP_K, tm), lambda i: (i, 0, 0)),
        out_shape=jax.ShapeDtypeStruct((nt, TOP_K, tm), I32),
        scratch_shapes=[pltpu.VMEM((1, LANES), F32)],
        compiler_params=_cparams(("arbitrary",)),
        name="route_positions",
    )(eidx, base)


def _scatter_kernel(pos_ref, *refs, tm, tile_starts):
    src_refs, xs_hbm, sem = refs[:-2], refs[-2], refs[-1]
    i = pl.program_id(0)

    def run(src_ref):
        def row_copy(t, k):
            return pltpu.make_async_copy(src_ref.at[pl.ds(t, 1)], xs_hbm.at[pl.ds(pos_ref[0, k, t], 1)], sem)

        def issue(t, c):
            for k in range(TOP_K):
                row_copy(t, k).start(priority=k)
            return c

        def drain(t, c):
            for k in range(TOP_K):
                row_copy(t, k).wait()
            return c

        lax.fori_loop(0, tm, issue, 0, unroll=8)
        lax.fori_loop(0, tm, drain, 0, unroll=8)

    for g, src_ref in enumerate(src_refs):
        pl.when((i >= tile_starts[g]) & (i < tile_starts[g + 1]))(functools.partial(run, src_ref))


def _scatter_rows(h1s, pos, *, tm):
    d = h1s[0].shape[1]
    tiles = [h.shape[0] // tm for h in h1s]
    tile_starts = [sum(tiles[:g]) for g in range(len(tiles) + 1)]
    nt = tile_starts[-1]

    def src_spec(g):
        return pl.BlockSpec((tm, d), lambda i: (jnp.clip(i - tile_starts[g], 0, tiles[g] - 1), 0))

    return pl.pallas_call(
        functools.partial(_scatter_kernel, tm=tm, tile_starts=tuple(tile_starts)),
        grid=(nt,),
        in_specs=[pl.BlockSpec((1, TOP_K, tm), lambda i: (i, 0, 0), memory_space=pltpu.SMEM)]
        + [src_spec(g) for g in range(len(h1s))],
        out_specs=pl.BlockSpec(memory_space=pl.ANY),
        out_shape=jax.ShapeDtypeStruct((TOP_K * nt * tm, d), h1s[0].dtype),
        scratch_shapes=[pltpu.SemaphoreType.DMA(())],
        compiler_params=_cparams(("arbitrary",)),
        name="scatter_to_experts",
    )(pos, *h1s)


def _expert_kernel(tile_ref, grp_ref, lo_ref, hi_ref, first_ref, x_ref, wg_ref, wu_ref, wd_ref, o_ref,
                   wgb, wub, wdb, *, tm):
    w = pl.program_id(0)
    changed = (w == 0) | (grp_ref[w] != grp_ref[jnp.maximum(w - 1, 0)])

    @pl.when(changed)
    def _():
        wgb[...] = wg_ref[0].astype(BF16)
        wub[...] = wu_ref[0].astype(BF16)
        wdb[...] = wd_ref[0].astype(BF16)

    xb = x_ref[...].astype(BF16)
    g = jnp.dot(xb, wgb[...], preferred_element_type=F32)
    u = jnp.dot(xb, wub[...], preferred_element_type=F32)
    hid = (g * _sigmoid(g) * u).astype(BF16)
    o = jnp.dot(hid, wdb[...], preferred_element_type=F32)
    row = tile_ref[w] * tm + lax.broadcasted_iota(I32, (tm, 1), 0)
    o = jnp.where((row >= lo_ref[w]) & (row < hi_ref[w]), o, 0.0)

    @pl.when(first_ref[w] == 1)
    def _():
        o_ref[...] = o

    @pl.when(first_ref[w] == 0)
    def _():
        o_ref[...] += o


def _expert_ffn(xs, w_gate, w_up, w_down, meta, *, tm):
    a, d = xs.shape
    n_experts, _, de = w_gate.shape
    tile_ids, grp_ids, lo, hi, first = meta
    n_work = tile_ids.shape[0]
    grid_spec = pltpu.PrefetchScalarGridSpec(
        num_scalar_prefetch=5,
        grid=(n_work,),
        in_specs=[pl.BlockSpec((tm, d), lambda w, ti, gi, lo_, hi_, fi: (ti[w], 0)),
                  pl.BlockSpec((1, d, de), lambda w, ti, gi, lo_, hi_, fi: (gi[w], 0, 0)),
                  pl.BlockSpec((1, d, de), lambda w, ti, gi, lo_, hi_, fi: (gi[w], 0, 0)),
                  pl.BlockSpec((1, de, d), lambda w, ti, gi, lo_, hi_, fi: (gi[w], 0, 0))],
        out_specs=pl.BlockSpec((tm, d), lambda w, ti, gi, lo_, hi_, fi: (ti[w], 0)),
        scratch_shapes=[pltpu.VMEM((d, de), BF16), pltpu.VMEM((d, de), BF16), pltpu.VMEM((de, d), BF16)],
    )
    return pl.pallas_call(
        functools.partial(_expert_kernel, tm=tm),
        grid_spec=grid_spec,
        out_shape=jax.ShapeDtypeStruct((a, d), F32),
        compiler_params=_cparams(("arbitrary",)),
        name="expert_ffn",
    )(tile_ids, grp_ids, lo, hi, first, xs, w_gate, w_up, w_down)


def _group_metadata(counts, *, n_rows, tm):
    n_experts = counts.shape[0]
    nt = n_rows // tm
    n_work = nt + n_experts - 1
    ends = jnp.cumsum(counts)
    starts = ends - counts
    ntiles_g = jnp.where(counts > 0, (ends - 1) // tm - starts // tm + 1, 0)
    work_end = jnp.cumsum(ntiles_g)
    work_start = work_end - ntiles_g
    total = work_end[-1]
    w = jnp.arange(n_work, dtype=I32)
    wc = jnp.minimum(w, total - 1)
    g = jnp.minimum(jnp.sum((work_end[None, :] <= wc[:, None]).astype(I32), axis=1), n_experts - 1)
    onehot = (g[:, None] == jnp.arange(n_experts, dtype=I32)[None, :]).astype(I32)
    pick = lambda a: jnp.sum(onehot * a[None, :], axis=1)
    starts_g, ends_g = pick(starts), pick(ends)
    tile = (starts_g // tm + (wc - pick(work_start))).astype(I32)
    valid = w < total
    lo = jnp.where(valid, jnp.maximum(starts_g, tile * tm), 0).astype(I32)
    hi = jnp.where(valid, jnp.minimum(ends_g, (tile + 1) * tm), 0).astype(I32)
    prev_tile = jnp.concatenate([jnp.full((1,), -1, I32), tile[:-1]])
    first = (tile != prev_tile).astype(I32)
    return tile, g, lo, hi, first


def _final_kernel(pos_ref, posn_ref, h1_ref, topw_ref, p_ref, os_hbm, l2g_ref, l2b_ref, wpg_ref, bpg_ref,
                  wpp_ref, y_ref, buf, sem, *, tm, alpha):
    i = pl.program_id(0)
    n = pl.num_programs(0)
    slot = i % 2

    def row_copy(pref, sl, t, k):
        return pltpu.make_async_copy(os_hbm.at[pl.ds(pref[0, k, t], 1)], buf.at[sl, k, pl.ds(t, 1)], sem.at[sl])

    def issue(pref, sl):
        def body(t, c):
            for k in range(TOP_K):
                row_copy(pref, sl, t, k).start(priority=k)
            return c
        lax.fori_loop(0, tm, body, 0, unroll=8)

    @pl.when(i == 0)
    def _():
        issue(pos_ref, 0)

    @pl.when(i + 1 < n)
    def _():
        issue(posn_ref, 1 - slot)

    def drain(t, c):
        for k in range(TOP_K):
            row_copy(pos_ref, slot, t, k).wait()
        return c

    lax.fori_loop(0, tm, drain, 0, unroll=8)
    emb = jnp.dot(p_ref[...].astype(BF16), wpp_ref[...], preferred_element_type=F32)
    y = topw_ref[:, 0:1] * buf[slot, 0] + topw_ref[:, 1:2] * buf[slot, 1]
    h2 = _ln(alpha * h1_ref[...] + y, l2g_ref[...], l2b_ref[...])
    gate = _sigmoid(jnp.dot(h2.astype(BF16), wpg_ref[...], preferred_element_type=F32) + bpg_ref[...])
    y_ref[...] = h2 + gate * emb


def _combine_ple(h1, topw, pos, p2d, out_sorted, l2g, l2b, w_pg_b, b_pg, w_pp_b, *, tm, alpha):
    t, d = h1.shape
    nt = t // tm
    ple = p2d.shape[1]
    tok = lambda w: pl.BlockSpec((tm, w), lambda i: (i, 0))
    return pl.pallas_call(
        functools.partial(_final_kernel, tm=tm, alpha=alpha),
        grid=(nt,),
        in_specs=[pl.BlockSpec((1, TOP_K, tm), lambda i: (i, 0, 0), memory_space=pltpu.SMEM),
                  pl.BlockSpec((1, TOP_K, tm), lambda i: (jnp.minimum(i + 1, nt - 1), 0, 0),
                               memory_space=pltpu.SMEM),
                  tok(d), tok(TOP_K), tok(ple), pl.BlockSpec(memory_space=pl.ANY),
                  _resident((1, d)), _resident((1, d)), _resident((d, d)), _resident((1, d)),
                  _resident((ple, d))],
        out_specs=tok(d),
        out_shape=jax.ShapeDtypeStruct((t, d), F32),
        scratch_shapes=[pltpu.VMEM((2, TOP_K, tm, d), F32), pltpu.SemaphoreType.DMA((2,))],
        compiler_params=_cparams(("arbitrary",)),
        name="combine_ple",
    )(pos, pos, h1, topw, p2d, out_sorted, l2g.reshape(1, d), l2b.reshape(1, d), w_pg_b, b_pg.reshape(1, d),
      w_pp_b)


def _post_mix(groups, prm, *, tm, alpha, n_experts, n_groups):
    routed = [_outproj_router(x2d, att, sgu, prm["ln_emb_g"], prm["ln_emb_b"], prm["w_o"], prm["ln1_g"],
                              prm["ln1_b"], prm["w_r"], prm["b_r"], tm=tm, alpha=alpha, n_experts=n_experts,
                              n_groups=n_groups) for x2d, att, sgu, _ in groups]
    t_all = sum(g[0].shape[0] for g in groups)
    counts = sum(cnt[0, :n_experts] for _, _, _, cnt in routed).astype(I32)
    starts = jnp.cumsum(counts) - counts
    base = jnp.pad(starts.astype(F32), (0, LANES - n_experts)).reshape(1, LANES)
    eidx_all = routed[0][1] if len(routed) == 1 else jnp.concatenate([r[1] for r in routed], axis=0)
    pos = _positions(eidx_all, base, tm=tm)
    xs = _scatter_rows([r[0] for r in routed], pos, tm=tm)
    meta = _group_metadata(counts, n_rows=TOP_K * t_all, tm=tm)
    out_sorted = _expert_ffn(xs, prm["w_gate"], prm["w_up"], prm["w_down"], meta, tm=tm)
    outs, tile0 = [], 0
    for (x2d, _, _, p2d), (h1, _, topw, _) in zip(groups, routed):
        tiles = x2d.shape[0] // tm
        outs.append(_combine_ple(h1, topw, pos[tile0:tile0 + tiles], p2d, out_sorted, prm["ln2_g"], prm["ln2_b"],
                                 prm["w_pg"], prm["b_pg"], prm["w_pp"], tm=tm, alpha=alpha))
        tile0 += tiles
    return outs


def kernel(x_prompt, x_sample, cache_k, cache_v, page_table, p_prompt, p_sample, ln_emb_g, ln_emb_b, w_in, lambda_q1, lambda_k1, lambda_q2, lambda_k2, subln_g, rel_bias, sgu_ln_g, sgu_ln_b, sgu_w, sgu_b, w_o, ln1_g, ln1_b, w_router_group, b_router_group, w_router_expert, b_router_expert, w_gate, w_up, w_down, ln2_g, ln2_b, w_ple_gate, b_ple_gate, w_ple_proj):
    bsz, seq, d = x_prompt.shape
    bd, tq_dec, _ = x_sample.shape
    depth, _, page, n_heads, vd = cache_v.shape
    hd = vd // 2
    dw = n_heads * vd
    sw = (w_in.shape[-1] - 3 * dw) // 2
    n_groups = w_router_group.shape[-1]
    n_experts = w_router_expert.shape[-1]
    n_buckets = rel_bias.shape[0]
    ple = p_prompt.shape[-1]
    alpha = (2.0 * depth) ** 0.25
    assert n_experts + n_groups <= LANES

    tp, ts = bsz * seq, bd * tq_dec
    tm_p = min(256, tp)
    tm_s = min(256, ts)
    tq = min(256, seq)
    n_pages = page_table.shape[1]
    pps = math.gcd(n_pages, 8)

    d_const = _const_bucket_distance(n_buckets)
    assert tq + 1 >= d_const and page + 1 >= d_const
    qi = np.arange(tq)[:, None]
    kj = np.arange(tq)[None, :]
    bkt_p = np.concatenate([np.where(qi <= kj, _t5_bucket(kj - qi, n_buckets), -1).astype(np.int32),
                            _t5_bucket(tq + kj - qi, n_buckets), np.full((tq, tq), n_buckets - 1, np.int32),
                            np.full((tq, tq), -1, np.int32)], axis=0)
    bias_p = _bias_tables(rel_bias, np.tile(bkt_p, (1, 2))).reshape(n_heads, 4, tq, 2 * tq)
    r2 = 2 * tq_dec
    di = np.tile(np.arange(tq_dec), 2)[:, None]
    dj = np.arange(page)[None, :]
    bkt_last = _t5_bucket(page + di - dj, n_buckets)
    bkt_new = np.where((dj <= di) & (dj < tq_dec), _t5_bucket(di - dj, n_buckets), -1).astype(np.int32)
    pen = _bias_tables(rel_bias, np.concatenate([bkt_last, bkt_new], axis=0))
    pen = pen.reshape(n_heads, 2, r2, page).transpose(1, 0, 2, 3)

    xp = x_prompt.reshape(tp, d)
    xs = x_sample.reshape(ts, d)
    hp_x, hs_x = xp, xs
    k_p_rows, v_p_rows, k_s_rows, v_s_rows, sgu_s_rows = [], [], [], [], []
    for l in range(depth):
        assert depth == 1, "the trunk input of deeper layers is the previous layer's output"
        lam_init = 0.8 - 0.6 * math.exp(-0.3 * l)
        lam_vecs = jnp.stack([lambda_q1[l], lambda_k1[l], lambda_q2[l], lambda_k2[l]])
        w_in_b = w_in[l].astype(BF16)
        pad_r = LANES - n_experts - n_groups
        prm = dict(
            ln_emb_g=ln_emb_g, ln_emb_b=ln_emb_b, w_o=w_o[l].astype(BF16), ln1_g=ln1_g[l], ln1_b=ln1_b[l],
            w_r=jnp.pad(jnp.concatenate([w_router_expert[l], w_router_group[l]], axis=1),
                        ((0, 0), (0, pad_r))).astype(BF16),
            b_r=jnp.pad(jnp.concatenate([b_router_expert[l], b_router_group[l]]), (0, pad_r)).reshape(1, LANES),
            w_gate=w_gate[l], w_up=w_up[l], w_down=w_down[l], ln2_g=ln2_g[l], ln2_b=ln2_b[l],
            w_pg=w_ple_gate[l].astype(BF16), b_pg=b_ple_gate[l], w_pp=w_ple_proj[l].astype(BF16))
        common = dict(dw=dw, sw=sw, n_heads=n_heads, q_scale=hd ** -0.5 * LOG2E)
        qp, kp, vp, kbp, vbp, sgu_p = _inproj(
            hp_x, ln_emb_g, ln_emb_b, w_in_b, sgu_ln_g[l], sgu_ln_b[l], sgu_w[l], sgu_b[l],
            seq=seq, tm=tm_p, q_dtype=BF16, emit_vnorm=False, **common)
        qs, ks, vs, _, _, sgu_s, vn_s = _inproj(
            hs_x, ln_emb_g, ln_emb_b, w_in_b, sgu_ln_g[l], sgu_ln_b[l], sgu_w[l], sgu_b[l],
            seq=tq_dec, tm=tm_s, q_dtype=F32, emit_vnorm=True, **common)
        att_p = _attn_prompt(qp.reshape(bsz, seq, dw), kbp.reshape(bsz, seq, dw), vbp.reshape(bsz, seq, dw),
                             bias_p, lam_vecs, subln_g[l], n_heads=n_heads, tq=tq, nh=math.gcd(n_heads, 4),
                             lam_init=lam_init)
        ck = (cache_k if depth == 1 else cache_k[l]).reshape(-1, page * n_heads, vd)
        cv = (cache_v if depth == 1 else cache_v[l]).reshape(-1, page * n_heads, vd)
        att_s = _attn_decode(qs, ks.reshape(-1, vd), vs.reshape(-1, vd), ck, cv, page_table, pen, lam_vecs,
                             subln_g[l], n_heads=n_heads, tq=tq_dec, pages_per_step=pps, lam_init=lam_init)
        post = dict(alpha=alpha, n_experts=n_experts, n_groups=n_groups)
        grp_p = (hp_x, att_p.reshape(tp, dw), sgu_p, p_prompt[l].reshape(tp, ple))
        grp_s = (hs_x, att_s, sgu_s, p_sample[l].reshape(ts, ple))
        if tm_p == tm_s:
            hp_x, hs_x = _post_mix([grp_p, grp_s], prm, tm=tm_p, **post)
        else:
            (hp_x,), (hs_x,) = _post_mix([grp_p], prm, tm=tm_p, **post), _post_mix([grp_s], prm, tm=tm_s, **post)
        k_p_rows.append(kp.reshape(bsz, seq, n_heads, vd))
        v_p_rows.append(vp.reshape(bsz, seq, n_heads, vd))
        k_s_rows.append(ks.reshape(bd, tq_dec, n_heads, vd))
        v_s_rows.append(vs.reshape(bd, tq_dec, n_heads, vd))
        sgu_s_rows.append(vn_s.reshape(bd, tq_dec, sw))
    stack = lambda rows: rows[0][None] if len(rows) == 1 else jnp.stack(rows)
    return (hp_x.reshape(bsz, seq, d), hs_x.reshape(bd, tq_dec, d), stack(k_p_rows), stack(v_p_rows),
            stack(k_s_rows), stack(v_s_rows), stack(sgu_s_rows))
```
